```python
import jax, jax.numpy as jnp
from jax import lax
import numpy as np

D_MODEL = 1024
BATCH = 2
SEQ = 8192
DEPTH = 4

N_MIXERS = 3
N_META = 16
D_FF = 2816
SC_WIDTH = 3
CONF_WIDTH = 31
N_HEADS = 16
HEAD_DIM = D_MODEL // N_HEADS
BLOCK = 128
EPS = 1e-6
MASK_VALUE = -1e30
N_A = (DEPTH + 2) // 3
N_B = (DEPTH + 1) // 3
N_C = DEPTH // 3

kernel_name = "hybrid_shortconv_conformer_fox_macaron"


def rms_norm(x, g):
    xf = x.astype(jnp.float32)
    y = xf * lax.rsqrt(jnp.mean(xf * xf, axis=-1, keepdims=True) + EPS)
    return (y * g.astype(jnp.float32)).astype(x.dtype)


def layer_norm(x, g, b):
    xf = x.astype(jnp.float32)
    mu = jnp.mean(xf, axis=-1, keepdims=True)
    xc = xf - mu
    var = jnp.mean(xc * xc, axis=-1, keepdims=True)
    y = xc * lax.rsqrt(var + EPS) * g.astype(jnp.float32) + b.astype(jnp.float32)
    return y.astype(x.dtype)


def causal_depthwise_conv(x, w):
    k = w.shape[0]
    return lax.conv_general_dilated(
        x, w[:, None, :].astype(x.dtype), window_strides=(1,), padding=[(k - 1, 0)],
        dimension_numbers=("NWC", "WIO", "NWC"), feature_group_count=x.shape[-1])


def swiglu(h, w_gate, w_up, w_down):
    return (jax.nn.silu(h @ w_gate) * (h @ w_up)) @ w_down


def short_conv_mixer(h, w_in, conv_w, w_out):
    b_gate, c_gate, v = jnp.split(h @ w_in, 3, axis=-1)
    y = b_gate * causal_depthwise_conv(c_gate * v, conv_w)
    return y @ w_out


def conformer_conv_mixer(h, w_in, conv_w, conv_b, ln_g, ln_b, w_out):
    a, g = jnp.split(h @ w_in, 2, axis=-1)
    u = a * jax.nn.sigmoid(g)
    u = causal_depthwise_conv(u, conv_w) + conv_b
    u = jax.nn.silu(layer_norm(u, ln_g, ln_b))
    return u @ w_out


def forgetting_attention(h, w_in, b_f, q_g, k_g, w_out):
    bsz, L, _ = h.shape
    proj = h @ w_in
    q = proj[..., :D_MODEL].reshape(bsz, L, N_HEADS, HEAD_DIM)
    k = proj[..., D_MODEL:2 * D_MODEL].reshape(bsz, L, N_HEADS, HEAD_DIM)
    v = proj[..., 2 * D_MODEL:3 * D_MODEL].reshape(bsz, L, N_HEADS, HEAD_DIM)
    f_logit = proj[..., 3 * D_MODEL:] + b_f
    q = rms_norm(q, q_g)
    k = rms_norm(k, k_g)
    log_f = jax.nn.log_sigmoid(f_logit.astype(jnp.float32))
    cum = jnp.cumsum(log_f, axis=1)
    pad = (-L) % BLOCK
    lp = L + pad
    n_blocks = lp // BLOCK
    pad4 = ((0, 0), (pad, 0), (0, 0), (0, 0))
    q = jnp.pad(q, pad4).transpose(0, 2, 1, 3)
    k = jnp.pad(k, pad4).transpose(0, 2, 1, 3)
    v = jnp.pad(v, pad4).transpose(0, 2, 1, 3)
    cum = jnp.pad(cum, ((0, 0), (pad, 0), (0, 0))).transpose(0, 2, 1)
    kpos = jnp.arange(lp)
    scale = HEAD_DIM ** -0.5

    def one_block(i):
        start = i * BLOCK
        qb = lax.dynamic_slice_in_dim(q, start, BLOCK, axis=2)
        cq = lax.dynamic_slice_in_dim(cum, start, BLOCK, axis=2)
        s = jnp.einsum("bhqd,bhkd->bhqk", qb, k, preferred_element_type=jnp.float32) * scale
        s = s + cq[..., :, None] - cum[..., None, :]
        qpos = start + jnp.arange(BLOCK)
        mask = (kpos[None, :] <= qpos[:, None]) & (kpos[None, :] >= pad)
        s = jnp.where(mask, s, MASK_VALUE)
        p = jax.nn.softmax(s, axis=-1).astype(v.dtype)
        return jnp.einsum("bhqk,bhkd->bhqd", p, v)

    o = lax.map(one_block, jnp.arange(n_blocks))
    o = o.transpose(1, 0, 3, 2, 4).reshape(bsz, lp, D_MODEL)[:, pad:]
    return o @ w_out


def setup_inputs(seed: int = 0) -> dict:
    key = jax.random.key(seed)
    ks = jax.random.split(key, 24)
    D, F, H = D_MODEL, D_FF, N_HEADS
    nrm = lambda k, shape, fan: jax.random.normal(k, shape, jnp.float32) * (fan ** -0.5)
    gain = lambda k, shape: 1.0 + 0.05 * jax.random.normal(k, shape, jnp.float32)
    small = lambda k, shape: 0.02 * jax.random.normal(k, shape, jnp.float32)
    return {
        "x": jax.random.normal(ks[0], (BATCH, SEQ, D), jnp.float32),
        "meta": jax.random.normal(ks[1], (N_META, D), jnp.float32),
        "ffn_norm": gain(ks[2], (DEPTH, 2, D)),
        "ffn_w_gate": nrm(ks[3], (DEPTH, 2, D, F), D),
        "ffn_w_up": nrm(ks[4], (DEPTH, 2, D, F), D),
        "ffn_w_down": nrm(ks[5], (DEPTH, 2, F, D), F),
        "mix_norm": gain(ks[6], (DEPTH, D)),
        "a_w_in": nrm(ks[7], (N_A, D, 3 * D), D),
        "a_conv": nrm(ks[8], (N_A, SC_WIDTH, D), SC_WIDTH),
        "a_w_out": nrm(ks[9], (N_A, D, D), D),
        "b_w_in": nrm(ks[10], (N_B, D, 2 * D), D),
        "b_conv": nrm(ks[11], (N_B, CONF_WIDTH, D), CONF_WIDTH),
        "b_conv_bias": small(ks[12], (N_B, D)),
        "b_ln_g": gain(ks[13], (N_B, D)),
        "b_ln_b": small(ks[14], (N_B, D)),
        "b_w_out": nrm(ks[15], (N_B, D, D), D),
        "c_w_in": nrm(ks[16], (N_C, D, 3 * D + H), D),
        "c_b_f": jax.random.uniform(ks[17], (N_C, H), jnp.float32, 1.0, 4.0),
        "c_q_norm": gain(ks[18], (N_C, HEAD_DIM)),
        "c_k_norm": gain(ks[19], (N_C, HEAD_DIM)),
        "c_w_out": nrm(ks[20], (N_C, D, D), D),
    }


def reference(x, meta, ffn_norm, ffn_w_gate, ffn_w_up, ffn_w_down, mix_norm,
              a_w_in, a_conv, a_w_out,
              b_w_in, b_conv, b_conv_bias, b_ln_g, b_ln_b, b_w_out,
              c_w_in, c_b_f, c_q_norm, c_k_norm, c_w_out):
    bsz = x.shape[0]
    meta_b = jnp.broadcast_to(meta[None].astype(x.dtype), (bsz, N_META, D_MODEL))
    h = jnp.concatenate([meta_b, x], axis=1)
    for i in range(DEPTH):
        m, j = i % N_MIXERS, i // N_MIXERS
        h = h + 0.5 * swiglu(rms_norm(h, ffn_norm[i, 0]), ffn_w_gate[i, 0], ffn_w_up[i, 0], ffn_w_down[i, 0])
        u = rms_norm(h, mix_norm[i])
        if m == 0:
            mix = short_conv_mixer(u, a_w_in[j], a_conv[j], a_w_out[j])
        elif m == 1:
            mix = conformer_conv_mixer(u, b_w_in[j], b_conv[j], b_conv_bias[j], b_ln_g[j], b_ln_b[j], b_w_out[j])
        else:
            mix = forgetting_attention(u, c_w_in[j], c_b_f[j], c_q_norm[j], c_k_norm[j], c_w_out[j])
        h = h + mix
        h = h + 0.5 * swiglu(rms_norm(h, ffn_norm[i, 1]), ffn_w_gate[i, 1], ffn_w_up[i, 1], ffn_w_down[i, 1])
    return h[:, N_META:]
```

```python
import functools

import jax
import jax.numpy as jnp
from jax import lax
from jax.experimental import pallas as pl
from jax.experimental.pallas import tpu as pltpu

D_MODEL = 1024
BATCH = 2
SEQ = 8192
DEPTH = 4
N_MIXERS = 3
N_META = 16
D_FF = 2816
SC_WIDTH = 3
CONF_WIDTH = 31
N_HEADS = 16
HEAD_DIM = D_MODEL // N_HEADS
EPS = 1e-6
MASK_VALUE = -1e30

LANES = 128
PAD = (-(N_META + SEQ)) % LANES
LP = PAD + N_META + SEQ
T = BATCH * LP
TM = 640
NT = LP // TM
HP = LANES
D_HP = N_HEADS * HP
N_SPLIT = 3
AUG_Q = HEAD_DIM
AUG_K = HEAD_DIM + N_SPLIT
ONE_V = HEAD_DIM
SC_HALO = 8
CONF_HALO = 32
CONF_ROWS = 64
VMEM_LIMIT = 56 * 1024 * 1024

F32 = jnp.float32
BF16 = jnp.bfloat16


def _rms(x, g):
    return x * lax.rsqrt(jnp.mean(x * x, axis=-1, keepdims=True) + EPS) * g


def _sigmoid(x):
    return 1.0 / (1.0 + jnp.exp(-x))


def _dot(a, b):
    return jnp.dot(a, b, preferred_element_type=F32)


def _split3(x):
    hi = x.astype(BF16)
    r1 = x - hi.astype(F32)
    mid = r1.astype(BF16)
    lo = (r1 - mid.astype(F32)).astype(BF16)
    return hi, mid, lo


def _row_in_batch(tile_in_batch):
    return tile_in_batch * TM + lax.broadcasted_iota(jnp.int32, (TM, 1), 0)


def _const_spec(shape):
    return pl.BlockSpec(shape, lambda *_: (0,) * len(shape), pipeline_mode=pl.Buffered(1))


def _row_spec(width):
    return pl.BlockSpec((TM, width), lambda i: (i, 0))


def _params(sem):
    return pltpu.CompilerParams(dimension_semantics=sem, vmem_limit_bytes=VMEM_LIMIT)


def _ffn_body(h_ref, g_ref, wg_ref, wu_ref, wd_ref, o_ref):
    h = h_ref[...]
    x = _rms(h, g_ref[...]).astype(BF16)
    gate = _dot(x, wg_ref[...])
    up = _dot(x, wu_ref[...])
    a = (gate * _sigmoid(gate) * up).astype(BF16)
    o_ref[...] = h + 0.5 * _dot(a, wd_ref[...])


def _ffn(h, g, wg, wu, wd):
    return pl.pallas_call(
        _ffn_body,
        grid=(T // TM,),
        in_specs=[_row_spec(D_MODEL), _const_spec((1, D_MODEL)),
                  _const_spec((D_MODEL, D_FF)), _const_spec((D_MODEL, D_FF)),
                  _const_spec((D_FF, D_MODEL))],
        out_specs=_row_spec(D_MODEL),
        out_shape=jax.ShapeDtypeStruct((T, D_MODEL), F32),
        compiler_params=_params(("arbitrary",)),
    )(h, g.reshape(1, D_MODEL), wg, wu, wd)


def _mixa_body(h_ref, g_ref, win_ref, cw_ref, wout_ref, o_ref, buf_ref):
    t = pl.program_id(0) % NT

    @pl.when(t == 0)
    def _():
        buf_ref[0:SC_HALO, :] = jnp.zeros((SC_HALO, D_MODEL), F32)

    h = h_ref[...]
    u = _rms(h, g_ref[...]).astype(BF16)
    p = _dot(u, win_ref[...])
    b_gate = p[:, :D_MODEL]
    cv = p[:, D_MODEL:2 * D_MODEL] * p[:, 2 * D_MODEL:]
    cv = jnp.where(_row_in_batch(t) >= PAD, cv, 0.0)
    buf_ref[SC_HALO:SC_HALO + TM, :] = cv
    y = cw_ref[SC_WIDTH - 1:SC_WIDTH, :] * cv
    for k in range(SC_WIDTH - 1):
        shift = SC_WIDTH - 1 - k
        y = y + cw_ref[k:k + 1, :] * buf_ref[SC_HALO - shift:SC_HALO - shift + TM, :]
    z = (b_gate * y).astype(BF16)
    o_ref[...] = h + _dot(z, wout_ref[...])
    buf_ref[0:SC_HALO, :] = buf_ref[TM:TM + SC_HALO, :]


def _mixa(h, g, w_in, conv_w, w_out):
    return pl.pallas_call(
        _mixa_body,
        grid=(T // TM,),
        in_specs=[_row_spec(D_MODEL), _const_spec((1, D_MODEL)),
                  _const_spec((D_MODEL, 3 * D_MODEL)), _const_spec((SC_WIDTH, D_MODEL)),
                  _const_spec((D_MODEL, D_MODEL))],
        out_specs=_row_spec(D_MODEL),
        out_shape=jax.ShapeDtypeStruct((T, D_MODEL), F32),
        scratch_shapes=[pltpu.VMEM((TM + SC_HALO, D_MODEL), F32)],
        compiler_params=_params(("arbitrary",)),
    )(h, g.reshape(1, D_MODEL), w_in, conv_w, w_out)


def _mixb_body(h_ref, g_ref, win_ref, cw_ref, cb_ref, lng_ref, lnb_ref, wout_ref,
               o_ref, buf_ref, y_ref):
    t = pl.program_id(0) % NT

    @pl.when(t == 0)
    def _():
        buf_ref[:, 0:CONF_HALO, :] = jnp.zeros((D_MODEL // LANES, CONF_HALO, LANES), F32)

    h = h_ref[...]
    u = _rms(h, g_ref[...]).astype(BF16)
    p = _dot(u, win_ref[...])
    glu = p[:, :D_MODEL] * _sigmoid(p[:, D_MODEL:])
    glu = jnp.where(_row_in_batch(t) >= PAD, glu, 0.0)
    n_col = D_MODEL // LANES
    for c in range(n_col):
        buf_ref[c, CONF_HALO:CONF_HALO + TM, :] = glu[:, c * LANES:(c + 1) * LANES]

    first = CONF_HALO - (CONF_WIDTH - 1)

    def lane_tile(c, carry):
        w = cw_ref[c]
        bias = jnp.broadcast_to(cb_ref[c], (CONF_ROWS, LANES))
        for r in range(0, TM, CONF_ROWS):
            acc = bias
            for k in range(CONF_WIDTH):
                acc = acc + w[k:k + 1, :] * buf_ref[c, r + first + k:r + first + k + CONF_ROWS, :]
            y_ref[c, r:r + CONF_ROWS, :] = acc
        buf_ref[c, 0:CONF_HALO, :] = buf_ref[c, TM:TM + CONF_HALO, :]
        return carry

    lax.fori_loop(0, n_col, lane_tile, 0)

    y = jnp.concatenate([y_ref[c] for c in range(n_col)], axis=1)
    mu = jnp.mean(y, axis=-1, keepdims=True)
    yc = y - mu
    var = jnp.mean(yc * yc, axis=-1, keepdims=True)
    ln = yc * lax.rsqrt(var + EPS) * lng_ref[...] + lnb_ref[...]
    s = (ln * _sigmoid(ln)).astype(BF16)
    o_ref[...] = h + _dot(s, wout_ref[...])


def _mixb(h, g, w_in, conv_w, conv_b, ln_g, ln_b, w_out):
    row = lambda v: v.reshape(1, D_MODEL)
    n_col = D_MODEL // LANES
    conv_w = conv_w.reshape(CONF_WIDTH, n_col, LANES).transpose(1, 0, 2)
    conv_b = conv_b.reshape(n_col, 1, LANES)
    return pl.pallas_call(
        _mixb_body,
        grid=(T // TM,),
        in_specs=[_row_spec(D_MODEL), _const_spec((1, D_MODEL)),
                  _const_spec((D_MODEL, 2 * D_MODEL)), _const_spec((n_col, CONF_WIDTH, LANES)),
                  _const_spec((n_col, 1, LANES)), _const_spec((1, D_MODEL)), _const_spec((1, D_MODEL)),
                  _const_spec((D_MODEL, D_MODEL))],
        out_specs=_row_spec(D_MODEL),
        out_shape=jax.ShapeDtypeStruct((T, D_MODEL), F32),
        scratch_shapes=[pltpu.VMEM((n_col, TM + CONF_HALO, LANES), F32),
                        pltpu.VMEM((n_col, TM, LANES), F32)],
        compiler_params=_params(("arbitrary",)),
    )(h, row(g), w_in, conv_w, conv_b, row(ln_g), row(ln_b), w_out)


def _fox_proj_body(h_ref, g_ref, wq_ref, wk_ref, wv_ref, wf_ref, bf_ref, gq_ref, gk_ref,
                   tri_ref, qa_ref, ka_ref, va_ref, carry_ref):
    t = pl.program_id(0) % NT

    @pl.when(t == 0)
    def _():
        carry_ref[...] = jnp.zeros_like(carry_ref)

    u = _rms(h_ref[...], g_ref[...]).astype(BF16)
    real = _row_in_batch(t) >= PAD

    f_logit = _dot(u, wf_ref[...]) + bf_ref[...]
    log_f = jnp.minimum(f_logit, 0.0) - jnp.log1p(jnp.exp(-jnp.abs(f_logit)))
    log_f = jnp.where(real, log_f, 0.0)
    c3 = _dot(tri_ref[...], jnp.concatenate(_split3(log_f), axis=1))
    cum = (c3[:, :LANES] + c3[:, LANES:2 * LANES] + c3[:, 2 * LANES:]) + carry_ref[0:1, :]
    carry_ref[...] = jnp.broadcast_to(cum[TM - 1:TM, :], carry_ref.shape)
    neg_ck = jnp.where(real, -cum, MASK_VALUE)

    xq = jnp.concatenate((u,) + _split3(cum), axis=1)
    xk = jnp.concatenate((u,) + _split3(neg_ck), axis=1)

    lane = lax.broadcasted_iota(jnp.int32, (1, HP), 1)
    is_head = lane < HEAD_DIM
    ones_q = ((lane >= AUG_K) & (lane < AUG_K + N_SPLIT)).astype(F32)
    ones_k = ((lane >= AUG_Q) & (lane < AUG_Q + N_SPLIT)).astype(F32)
    ones_v = (lane == ONE_V).astype(F32)
    gq = gq_ref[...] * (HEAD_DIM ** -0.5)
    gk = gk_ref[...]

    def normed(p, gain):
        ph = jnp.where(is_head, p, 0.0)
        ms = jnp.sum(ph * ph, axis=-1, keepdims=True) * (1.0 / HEAD_DIM)
        return ph * lax.rsqrt(ms + EPS) * gain + (p - ph)

    pair = 2 * HP
    for j in range(N_HEADS // 2):
        pq = _dot(xq, wq_ref[:, j * pair:(j + 1) * pair])
        pk = _dot(xk, wk_ref[:, j * pair:(j + 1) * pair])
        pv = _dot(u, wv_ref[:, j * pair:(j + 1) * pair])
        for e in range(2):
            src = slice(e * HP, (e + 1) * HP)
            dst = slice(j * pair + e * HP, j * pair + (e + 1) * HP)
            qa_ref[:, dst] = (normed(pq[:, src], gq) + ones_q).astype(BF16)
            ka_ref[:, dst] = (normed(pk[:, src], gk) + ones_k).astype(BF16)
            va_ref[:, dst] = (pv[:, src] + ones_v).astype(BF16)


def _fox_proj(h, g, wq, wk, wv, wf, bf, gq, gk, tri):
    k_aug = D_MODEL + N_SPLIT * LANES
    out = jax.ShapeDtypeStruct((T, D_HP), BF16)
    return pl.pallas_call(
        _fox_proj_body,
        grid=(T // TM,),
        in_specs=[_row_spec(D_MODEL), _const_spec((1, D_MODEL)),
                  _const_spec((k_aug, D_HP)), _const_spec((k_aug, D_HP)),
                  _const_spec((D_MODEL, D_HP)), _const_spec((D_MODEL, LANES)),
                  _const_spec((1, LANES)), _const_spec((1, HP)), _const_spec((1, HP)),
                  _const_spec((TM, TM))],
        out_specs=[_row_spec(D_HP)] * 3,
        out_shape=[out] * 3,
        scratch_shapes=[pltpu.VMEM((8, LANES), F32)],
        compiler_params=_params(("arbitrary",)),
    )(h, g.reshape(1, D_MODEL), wq, wk, wv, wf, bf, gq, gk, tri)


def _fox_attn_body(q_ref, k_ref, v_ref, o_ref):
    i = pl.program_id(2)
    q = q_ref[0]

    def block(kb, m, acc, causal):
        start = pl.multiple_of(kb * TM, TM)
        k = k_ref[0, pl.ds(start, TM), :]
        v = v_ref[0, pl.ds(start, TM), :]
        s = lax.dot_general(q, k, (((1,), (1,)), ((), ())), preferred_element_type=F32)
        if causal:
            row = lax.broadcasted_iota(jnp.int32, (TM, TM), 0)
            col = lax.broadcasted_iota(jnp.int32, (TM, TM), 1)
            s = jnp.where(col <= row, s, MASK_VALUE)
        m_new = jnp.maximum(m, jnp.max(s, axis=-1, keepdims=True))
        p = jnp.exp(s - m_new).astype(BF16)
        acc = jnp.exp(m - m_new) * acc + _dot(p, v)
        return m_new, acc

    m0 = jnp.full((TM, 1), MASK_VALUE, F32)
    acc0 = jnp.zeros((TM, HP), F32)
    m, acc = lax.fori_loop(0, i, lambda kb, c: block(kb, c[0], c[1], False), (m0, acc0))
    m, acc = block(i, m, acc, True)
    o_ref[0] = (acc * (1.0 / acc[:, ONE_V:ONE_V + 1])).astype(BF16)


def _fox_attn(qa, ka, va):
    shape3 = (BATCH, LP, D_HP)
    q_spec = pl.BlockSpec((1, TM, HP), lambda b, h, i: (b, i, h))
    kv_spec = pl.BlockSpec((1, LP, HP), lambda b, h, i: (b, 0, h))
    return pl.pallas_call(
        _fox_attn_body,
        grid=(BATCH, N_HEADS, NT),
        in_specs=[q_spec, kv_spec, kv_spec],
        out_specs=q_spec,
        out_shape=jax.ShapeDtypeStruct(shape3, BF16),
        compiler_params=_params(("arbitrary", "arbitrary", "arbitrary")),
    )(qa.reshape(shape3), ka.reshape(shape3), va.reshape(shape3)).reshape(T, D_HP)


def _fox_out_body(h_ref, o_ref, w_ref, out_ref):
    out_ref[...] = h_ref[...] + _dot(o_ref[...], w_ref[...])


def _fox_out(h, o, w_out):
    return pl.pallas_call(
        _fox_out_body,
        grid=(T // TM,),
        in_specs=[_row_spec(D_MODEL), _row_spec(D_HP), _const_spec((D_HP, D_MODEL))],
        out_specs=_row_spec(D_MODEL),
        out_shape=jax.ShapeDtypeStruct((T, D_MODEL), F32),
        compiler_params=_params(("arbitrary",)),
    )(h, o, w_out)


def _pad_heads(w):
    rows = w.shape[0]
    w = w.reshape(rows, N_HEADS, HEAD_DIM)
    return jnp.pad(w, ((0, 0), (0, 0), (0, HP - HEAD_DIM))).reshape(rows, D_HP)


def _aug_rows(first_lane):
    r = jnp.arange(N_SPLIT * LANES)
    piece, head = r // LANES, r % LANES
    col = head * HP + first_lane + piece
    hit = (jnp.arange(D_HP)[None, :] == col[:, None]) & (head < N_HEADS)[:, None]
    return hit.astype(BF16)


def _fox(h, g, w_in, b_f, q_g, k_g, w_out):
    wq = jnp.concatenate([_pad_heads(w_in[:, :D_MODEL]).astype(BF16), _aug_rows(AUG_Q)], axis=0)
    wk = jnp.concatenate([_pad_heads(w_in[:, D_MODEL:2 * D_MODEL]).astype(BF16), _aug_rows(AUG_K)], axis=0)
    wv = _pad_heads(w_in[:, 2 * D_MODEL:3 * D_MODEL]).astype(BF16)
    wf = jnp.pad(w_in[:, 3 * D_MODEL:], ((0, 0), (0, LANES - N_HEADS))).astype(BF16)
    bf = jnp.pad(b_f, (0, LANES - N_HEADS)).reshape(1, LANES)
    gq = jnp.pad(q_g, (0, HP - HEAD_DIM)).reshape(1, HP)
    gk = jnp.pad(k_g, (0, HP - HEAD_DIM)).reshape(1, HP)
    tri = (jnp.arange(TM)[None, :] <= jnp.arange(TM)[:, None]).astype(BF16)
    qa, ka, va = _fox_proj(h, g, wq, wk, wv, wf, bf, gq, gk, tri)
    o = _fox_attn(qa, ka, va)
    wo = jnp.pad(w_out.reshape(N_HEADS, HEAD_DIM, D_MODEL),
                 ((0, 0), (0, HP - HEAD_DIM), (0, 0))).reshape(D_HP, D_MODEL).astype(BF16)
    return _fox_out(h, o, wo)


def kernel(x, meta, ffn_norm, ffn_w_gate, ffn_w_up, ffn_w_down, mix_norm, a_w_in, a_conv, a_w_out, b_w_in, b_conv, b_conv_bias, b_ln_g, b_ln_b, b_w_out, c_w_in, c_b_f, c_q_norm, c_k_norm, c_w_out):
    bsz = x.shape[0]
    meta_b = jnp.broadcast_to(meta[None].astype(x.dtype), (bsz, N_META, D_MODEL))
    h = jnp.concatenate([jnp.zeros((bsz, PAD, D_MODEL), x.dtype), meta_b, x], axis=1)
    h = h.reshape(T, D_MODEL)
    for i in range(DEPTH):
        m, j = i % N_MIXERS, i // N_MIXERS
        h = _ffn(h, ffn_norm[i, 0], ffn_w_gate[i, 0].astype(BF16), ffn_w_up[i, 0].astype(BF16),
                 ffn_w_down[i, 0].astype(BF16))
        if m == 0:
            h = _mixa(h, mix_norm[i], a_w_in[j].astype(BF16), a_conv[j], a_w_out[j].astype(BF16))
        elif m == 1:
            h = _mixb(h, mix_norm[i], b_w_in[j].astype(BF16), b_conv[j], b_conv_bias[j],
                      b_ln_g[j], b_ln_b[j], b_w_out[j].astype(BF16))
        else:
            h = _fox(h, mix_norm[i], c_w_in[j], c_b_f[j], c_q_norm[j], c_k_norm[j], c_w_out[j])
        h = _ffn(h, ffn_norm[i, 1], ffn_w_gate[i, 1].astype(BF16), ffn_w_up[i, 1].astype(BF16),
                 ffn_w_down[i, 1].astype(BF16))
    return h.reshape(bsz, LP, D_MODEL)[:, PAD + N_META:]
```

```python
import functools

import jax
import jax.numpy as jnp
from jax import lax
from jax.experimental import pallas as pl
from jax.experimental.pallas import tpu as pltpu

D_MODEL = 1024
BATCH = 2
SEQ = 8192
DEPTH = 4
N_MIXERS = 3
N_META = 16
D_FF = 2816
SC_WIDTH = 3
CONF_WIDTH = 31
N_HEADS = 16
HEAD_DIM = D_MODEL // N_HEADS
EPS = 1e-6
MASK_VALUE = -1e30
LOG2E = 1.4426950408889634

LANES = 128
PAD = (-(N_META + SEQ)) % LANES
LP = PAD + N_META + SEQ
T = BATCH * LP
TM = 640
NT = LP // TM
HP = LANES
D_HP = N_HEADS * HP
N_SPLIT = 3
AUG_Q = HEAD_DIM
AUG_K = HEAD_DIM + N_SPLIT
ONE_V = HEAD_DIM
SC_HALO = 8
CONF_HALO = 32
CONF_ROWS = 64
VMEM_LIMIT = 56 * 1024 * 1024

F32 = jnp.float32
BF16 = jnp.bfloat16


def _rms(x, g):
    return x * lax.rsqrt(jnp.mean(x * x, axis=-1, keepdims=True) + EPS) * g


def _sigmoid(x):
    return 1.0 / (1.0 + jnp.exp(-x))


def _dot(a, b):
    return jnp.dot(a, b, preferred_element_type=F32)


def _split3(x):
    hi = x.astype(BF16)
    r1 = x - hi.astype(F32)
    mid = r1.astype(BF16)
    lo = (r1 - mid.astype(F32)).astype(BF16)
    return hi, mid, lo


def _row_in_batch(tile_in_batch):
    return tile_in_batch * TM + lax.broadcasted_iota(jnp.int32, (TM, 1), 0)


def _const_spec(shape):
    return pl.BlockSpec(shape, lambda *_: (0,) * len(shape), pipeline_mode=pl.Buffered(1))


def _row_spec(width):
    return pl.BlockSpec((TM, width), lambda i: (i, 0))


def _params(sem):
    return pltpu.CompilerParams(dimension_semantics=sem, vmem_limit_bytes=VMEM_LIMIT)


def _ffn_body(h_ref, g_ref, wg_ref, wu_ref, wd_ref, o_ref):
    h = h_ref[...]
    x = _rms(h, g_ref[...]).astype(BF16)
    gate = _dot(x, wg_ref[...])
    up = _dot(x, wu_ref[...])
    a = (gate * _sigmoid(gate) * up).astype(BF16)
    o_ref[...] = h + 0.5 * _dot(a, wd_ref[...])


def _ffn(h, g, wg, wu, wd):
    return pl.pallas_call(
        _ffn_body,
        name="ffn",
        grid=(T // TM,),
        in_specs=[_row_spec(D_MODEL), _const_spec((1, D_MODEL)),
                  _const_spec((D_MODEL, D_FF)), _const_spec((D_MODEL, D_FF)),
                  _const_spec((D_FF, D_MODEL))],
        out_specs=_row_spec(D_MODEL),
        out_shape=jax.ShapeDtypeStruct((T, D_MODEL), F32),
        compiler_params=_params(("arbitrary",)),
    )(h, g.reshape(1, D_MODEL), wg, wu, wd)


def _mixa_body(h_ref, g_ref, win_ref, cw_ref, wout_ref, o_ref, buf_ref):
    t = pl.program_id(0) % NT

    @pl.when(t == 0)
    def _():
        buf_ref[0:SC_HALO, :] = jnp.zeros((SC_HALO, D_MODEL), F32)

    h = h_ref[...]
    u = _rms(h, g_ref[...]).astype(BF16)
    p = _dot(u, win_ref[...])
    b_gate = p[:, :D_MODEL]
    cv = p[:, D_MODEL:2 * D_MODEL] * p[:, 2 * D_MODEL:]
    cv = jnp.where(_row_in_batch(t) >= PAD, cv, 0.0)
    buf_ref[SC_HALO:SC_HALO + TM, :] = cv
    y = cw_ref[SC_WIDTH - 1:SC_WIDTH, :] * cv
    for k in range(SC_WIDTH - 1):
        shift = SC_WIDTH - 1 - k
        y = y + cw_ref[k:k + 1, :] * buf_ref[SC_HALO - shift:SC_HALO - shift + TM, :]
    z = (b_gate * y).astype(BF16)
    o_ref[...] = h + _dot(z, wout_ref[...])
    buf_ref[0:SC_HALO, :] = buf_ref[TM:TM + SC_HALO, :]


def _mixa(h, g, w_in, conv_w, w_out):
    return pl.pallas_call(
        _mixa_body,
        name="short_conv_mixer",
        grid=(T // TM,),
        in_specs=[_row_spec(D_MODEL), _const_spec((1, D_MODEL)),
                  _const_spec((D_MODEL, 3 * D_MODEL)), _const_spec((SC_WIDTH, D_MODEL)),
                  _const_spec((D_MODEL, D_MODEL))],
        out_specs=_row_spec(D_MODEL),
        out_shape=jax.ShapeDtypeStruct((T, D_MODEL), F32),
        scratch_shapes=[pltpu.VMEM((TM + SC_HALO, D_MODEL), F32)],
        compiler_params=_params(("arbitrary",)),
    )(h, g.reshape(1, D_MODEL), w_in, conv_w, w_out)


def _mixb_body(h_ref, g_ref, win_ref, cw_ref, cb_ref, lng_ref, lnb_ref, wout_ref,
               o_ref, buf_ref, y_ref):
    t = pl.program_id(0) % NT

    @pl.when(t == 0)
    def _():
        buf_ref[:, 0:CONF_HALO, :] = jnp.zeros((D_MODEL // LANES, CONF_HALO, LANES), F32)

    h = h_ref[...]
    u = _rms(h, g_ref[...]).astype(BF16)
    p = _dot(u, win_ref[...])
    glu = p[:, :D_MODEL] * _sigmoid(p[:, D_MODEL:])
    glu = jnp.where(_row_in_batch(t) >= PAD, glu, 0.0)
    n_col = D_MODEL // LANES
    for c in range(n_col):
        buf_ref[c, CONF_HALO:CONF_HALO + TM, :] = glu[:, c * LANES:(c + 1) * LANES]

    first = CONF_HALO - (CONF_WIDTH - 1)

    def lane_tile(c, carry):
        w = cw_ref[c]
        bias = jnp.broadcast_to(cb_ref[c], (CONF_ROWS, LANES))
        for r in range(0, TM, CONF_ROWS):
            acc = bias
            for k in range(CONF_WIDTH):
                acc = acc + w[k:k + 1, :] * buf_ref[c, r + first + k:r + first + k + CONF_ROWS, :]
            y_ref[c, r:r + CONF_ROWS, :] = acc
        buf_ref[c, 0:CONF_HALO, :] = buf_ref[c, TM:TM + CONF_HALO, :]
        return carry

    lax.fori_loop(0, n_col, lane_tile, 0)

    y = jnp.concatenate([y_ref[c] for c in range(n_col)], axis=1)
    mu = jnp.mean(y, axis=-1, keepdims=True)
    yc = y - mu
    var = jnp.mean(yc * yc, axis=-1, keepdims=True)
    ln = yc * lax.rsqrt(var + EPS) * lng_ref[...] + lnb_ref[...]
    s = (ln * _sigmoid(ln)).astype(BF16)
    o_ref[...] = h + _dot(s, wout_ref[...])


def _mixb(h, g, w_in, conv_w, conv_b, ln_g, ln_b, w_out):
    row = lambda v: v.reshape(1, D_MODEL)
    n_col = D_MODEL // LANES
    conv_w = conv_w.reshape(CONF_WIDTH, n_col, LANES).transpose(1, 0, 2)
    conv_b = conv_b.reshape(n_col, 1, LANES)
    return pl.pallas_call(
        _mixb_body,
        name="conformer_mixer",
        grid=(T // TM,),
        in_specs=[_row_spec(D_MODEL), _const_spec((1, D_MODEL)),
                  _const_spec((D_MODEL, 2 * D_MODEL)), _const_spec((n_col, CONF_WIDTH, LANES)),
                  _const_spec((n_col, 1, LANES)), _const_spec((1, D_MODEL)), _const_spec((1, D_MODEL)),
                  _const_spec((D_MODEL, D_MODEL))],
        out_specs=_row_spec(D_MODEL),
        out_shape=jax.ShapeDtypeStruct((T, D_MODEL), F32),
        scratch_shapes=[pltpu.VMEM((n_col, TM + CONF_HALO, LANES), F32),
                        pltpu.VMEM((n_col, TM, LANES), F32)],
        compiler_params=_params(("arbitrary",)),
    )(h, row(g), w_in, conv_w, conv_b, row(ln_g), row(ln_b), w_out)


def _fox_proj_body(h_ref, g_ref, wq_ref, wk_ref, wv_ref, wf_ref, bf_ref, gq_ref, gk_ref,
                   tri_ref, qa_ref, ka_ref, va_ref, carry_ref):
    t = pl.program_id(0) % NT

    @pl.when(t == 0)
    def _():
        carry_ref[...] = jnp.zeros_like(carry_ref)

    u = _rms(h_ref[...], g_ref[...]).astype(BF16)
    real = _row_in_batch(t) >= PAD

    f_logit = _dot(u, wf_ref[...]) + bf_ref[...]
    log_f = jnp.minimum(f_logit, 0.0) - jnp.log1p(jnp.exp(-jnp.abs(f_logit)))
    log_f = jnp.where(real, log_f, 0.0)
    c3 = _dot(tri_ref[...], jnp.concatenate(_split3(log_f), axis=1))
    cum = (c3[:, :LANES] + c3[:, LANES:2 * LANES] + c3[:, 2 * LANES:]) + carry_ref[0:1, :]
    carry_ref[...] = jnp.broadcast_to(cum[TM - 1:TM, :], carry_ref.shape)
    neg_ck = jnp.where(real, -LOG2E * cum, MASK_VALUE)

    xq = jnp.concatenate((u,) + _split3(LOG2E * cum), axis=1)
    xk = jnp.concatenate((u,) + _split3(neg_ck), axis=1)

    lane = lax.broadcasted_iota(jnp.int32, (1, HP), 1)
    is_head = lane < HEAD_DIM
    ones_q = ((lane >= AUG_K) & (lane < AUG_K + N_SPLIT)).astype(F32)
    ones_k = ((lane >= AUG_Q) & (lane < AUG_Q + N_SPLIT)).astype(F32)
    ones_v = (lane == ONE_V).astype(F32)
    gq = gq_ref[...] * (LOG2E * HEAD_DIM ** -0.5)
    gk = gk_ref[...]

    def normed(p, gain):
        ph = jnp.where(is_head, p, 0.0)
        ms = jnp.sum(ph * ph, axis=-1, keepdims=True) * (1.0 / HEAD_DIM)
        return ph * lax.rsqrt(ms + EPS) * gain + (p - ph)

    pair = 2 * HP
    for j in range(N_HEADS // 2):
        pq = _dot(xq, wq_ref[:, j * pair:(j + 1) * pair])
        pk = _dot(xk, wk_ref[:, j * pair:(j + 1) * pair])
        pv = _dot(u, wv_ref[:, j * pair:(j + 1) * pair])
        for e in range(2):
            src = slice(e * HP, (e + 1) * HP)
            dst = slice(j * pair + e * HP, j * pair + (e + 1) * HP)
            qa_ref[:, dst] = (normed(pq[:, src], gq) + ones_q).astype(BF16)
            ka_ref[:, dst] = (normed(pk[:, src], gk) + ones_k).astype(BF16)
            va_ref[:, dst] = (pv[:, src] + ones_v).astype(BF16)


def _fox_proj(h, g, wq, wk, wv, wf, bf, gq, gk, tri):
    k_aug = D_MODEL + N_SPLIT * LANES
    out = jax.ShapeDtypeStruct((T, D_HP), BF16)
    return pl.pallas_call(
        _fox_proj_body,
        name="fox_proj",
        grid=(T // TM,),
        in_specs=[_row_spec(D_MODEL), _const_spec((1, D_MODEL)),
                  _const_spec((k_aug, D_HP)), _const_spec((k_aug, D_HP)),
                  _const_spec((D_MODEL, D_HP)), _const_spec((D_MODEL, LANES)),
                  _const_spec((1, LANES)), _const_spec((1, HP)), _const_spec((1, HP)),
                  _const_spec((TM, TM))],
        out_specs=[_row_spec(D_HP)] * 3,
        out_shape=[out] * 3,
        scratch_shapes=[pltpu.VMEM((8, LANES), F32)],
        compiler_params=_params(("arbitrary",)),
    )(h, g.reshape(1, D_MODEL), wq, wk, wv, wf, bf, gq, gk, tri)


def _fox_attn_body(q_ref, k_ref, v_ref, o_ref, sa_ref, sb_ref, m_ref, acc_ref):
    i = pl.program_id(2)
    q = q_ref[0]

    def rows(kb):
        return pl.ds(pl.multiple_of(kb * TM, TM), TM)

    def scores(kb):
        return lax.dot_general(q, k_ref[0, rows(kb), :], (((1,), (1,)), ((), ())),
                               preferred_element_type=F32)

    def absorb(s_ref, kb):
        s = s_ref[...]
        m_old = m_ref[...]
        m_new = jnp.maximum(m_old, jnp.max(s, axis=-1, keepdims=True))
        p = jnp.exp2(s - m_new).astype(BF16)
        acc_ref[...] = jnp.exp2(m_old - m_new) * acc_ref[...] + _dot(p, v_ref[0, rows(kb), :])
        m_ref[...] = m_new

    def half(s_in, s_out, kb_next, kb_in):
        s_out[...] = scores(kb_next)
        absorb(s_in, kb_in)

    row = lax.broadcasted_iota(jnp.int32, (TM, TM), 0)
    col = lax.broadcasted_iota(jnp.int32, (TM, TM), 1)
    sa_ref[...] = jnp.where(col <= row, scores(i), MASK_VALUE)
    m_ref[...] = jnp.full(m_ref.shape, MASK_VALUE, F32)
    acc_ref[...] = jnp.zeros(acc_ref.shape, F32)

    def pair(u, kb_in):
        half(sa_ref, sb_ref, 2 * u, kb_in)
        half(sb_ref, sa_ref, 2 * u + 1, 2 * u)
        return 2 * u + 1

    kb_a = lax.fori_loop(0, i // 2, pair, i)

    @pl.when(i % 2 == 1)
    def _():
        half(sa_ref, sb_ref, i - 1, kb_a)
        absorb(sb_ref, i - 1)

    @pl.when(i % 2 == 0)
    def _():
        absorb(sa_ref, kb_a)

    acc = acc_ref[...]
    o_ref[0] = (acc * (1.0 / acc[:, ONE_V:ONE_V + 1])).astype(BF16)


def _fox_attn(qa, ka, va):
    shape3 = (BATCH, LP, D_HP)
    q_spec = pl.BlockSpec((1, TM, HP), lambda b, h, i: (b, i, h))
    kv_spec = pl.BlockSpec((1, LP, HP), lambda b, h, i: (b, 0, h))
    return pl.pallas_call(
        _fox_attn_body,
        name="fox_attn",
        grid=(BATCH, N_HEADS, NT),
        in_specs=[q_spec, kv_spec, kv_spec],
        out_specs=q_spec,
        out_shape=jax.ShapeDtypeStruct(shape3, BF16),
        scratch_shapes=[pltpu.VMEM((TM, TM), F32), pltpu.VMEM((TM, TM), F32),
                        pltpu.VMEM((TM, 1), F32), pltpu.VMEM((TM, HP), F32)],
        compiler_params=_params(("arbitrary", "arbitrary", "arbitrary")),
    )(qa.reshape(shape3), ka.reshape(shape3), va.reshape(shape3)).reshape(T, D_HP)


def _fox_out_body(h_ref, o_ref, w_ref, out_ref):
    out_ref[...] = h_ref[...] + _dot(o_ref[...], w_ref[...])


def _fox_out(h, o, w_out):
    return pl.pallas_call(
        _fox_out_body,
        name="fox_out",
        grid=(T // TM,),
        in_specs=[_row_spec(D_MODEL), _row_spec(D_HP), _const_spec((D_HP, D_MODEL))],
        out_specs=_row_spec(D_MODEL),
        out_shape=jax.ShapeDtypeStruct((T, D_MODEL), F32),
        compiler_params=_params(("arbitrary",)),
    )(h, o, w_out)


def _pad_heads(w):
    rows = w.shape[0]
    w = w.reshape(rows, N_HEADS, HEAD_DIM)
    return jnp.pad(w, ((0, 0), (0, 0), (0, HP - HEAD_DIM))).reshape(rows, D_HP)


def _aug_rows(first_lane):
    r = jnp.arange(N_SPLIT * LANES)
    piece, head = r // LANES, r % LANES
    col = head * HP + first_lane + piece
    hit = (jnp.arange(D_HP)[None, :] == col[:, None]) & (head < N_HEADS)[:, None]
    return hit.astype(BF16)


def _fox(h, g, w_in, b_f, q_g, k_g, w_out):
    wq = jnp.concatenate([_pad_heads(w_in[:, :D_MODEL]).astype(BF16), _aug_rows(AUG_Q)], axis=0)
    wk = jnp.concatenate([_pad_heads(w_in[:, D_MODEL:2 * D_MODEL]).astype(BF16), _aug_rows(AUG_K)], axis=0)
    wv = _pad_heads(w_in[:, 2 * D_MODEL:3 * D_MODEL]).astype(BF16)
    wf = jnp.pad(w_in[:, 3 * D_MODEL:], ((0, 0), (0, LANES - N_HEADS))).astype(BF16)
    bf = jnp.pad(b_f, (0, LANES - N_HEADS)).reshape(1, LANES)
    gq = jnp.pad(q_g, (0, HP - HEAD_DIM)).reshape(1, HP)
    gk = jnp.pad(k_g, (0, HP - HEAD_DIM)).reshape(1, HP)
    tri = (jnp.arange(TM)[None, :] <= jnp.arange(TM)[:, None]).astype(BF16)
    qa, ka, va = _fox_proj(h, g, wq, wk, wv, wf, bf, gq, gk, tri)
    o = _fox_attn(qa, ka, va)
    wo = jnp.pad(w_out.reshape(N_HEADS, HEAD_DIM, D_MODEL),
                 ((0, 0), (0, HP - HEAD_DIM), (0, 0))).reshape(D_HP, D_MODEL).astype(BF16)
    return _fox_out(h, o, wo)


def kernel(x, meta, ffn_norm, ffn_w_gate, ffn_w_up, ffn_w_down, mix_norm, a_w_in, a_conv, a_w_out, b_w_in, b_conv, b_conv_bias, b_ln_g, b_ln_b, b_w_out, c_w_in, c_b_f, c_q_norm, c_k_norm, c_w_out):
    bsz = x.shape[0]
    meta_b = jnp.broadcast_to(meta[None].astype(x.dtype), (bsz, N_META, D_MODEL))
    h = jnp.concatenate([jnp.zeros((bsz, PAD, D_MODEL), x.dtype), meta_b, x], axis=1)
    h = h.reshape(T, D_MODEL)
    for i in range(DEPTH):
        m, j = i % N_MIXERS, i // N_MIXERS
        h = _ffn(h, ffn_norm[i, 0], ffn_w_gate[i, 0].astype(BF16), ffn_w_up[i, 0].astype(BF16),
                 ffn_w_down[i, 0].astype(BF16))
        if m == 0:
            h = _mixa(h, mix_norm[i], a_w_in[j].astype(BF16), a_conv[j], a_w_out[j].astype(BF16))
        elif m == 1:
            h = _mixb(h, mix_norm[i], b_w_in[j].astype(BF16), b_conv[j], b_conv_bias[j],
                      b_ln_g[j], b_ln_b[j], b_w_out[j].astype(BF16))
        else:
            h = _fox(h, mix_norm[i], c_w_in[j], c_b_f[j], c_q_norm[j], c_k_norm[j], c_w_out[j])
        h = _ffn(h, ffn_norm[i, 1], ffn_w_gate[i, 1].astype(BF16), ffn_w_up[i, 1].astype(BF16),
                 ffn_w_down[i, 1].astype(BF16))
    return h.reshape(bsz, LP, D_MODEL)[:, PAD + N_META:]
```

```python
import functools

import jax
import jax.numpy as jnp
from jax import lax
from jax.experimental import pallas as pl
from jax.experimental.pallas import tpu as pltpu

D_MODEL = 1024
BATCH = 2
SEQ = 8192
DEPTH = 4
N_MIXERS = 3
N_META = 16
D_FF = 2816
SC_WIDTH = 3
CONF_WIDTH = 31
N_HEADS = 16
HEAD_DIM = D_MODEL // N_HEADS
EPS = 1e-6
MASK_VALUE = -1e30
LOG2E = 1.4426950408889634

LANES = 128
PAD = (-(N_META + SEQ)) % LANES
LP = PAD + N_META + SEQ
T = BATCH * LP
TM = 640
NT = LP // TM
N_PAIRS = N_HEADS // 2
PAIR_W = 2 * LANES
D_X = N_PAIRS * PAIR_W
N_SPLIT = 3
AUG_STRIDE = 8
ATTN_PAIRS = 2
SC_HALO = 8
CONF_HALO = 32
CONF_ROWS = 64
VMEM_LIMIT = 56 * 1024 * 1024

F32 = jnp.float32
BF16 = jnp.bfloat16


def _rms(x, g):
    return x * lax.rsqrt(jnp.mean(x * x, axis=-1, keepdims=True) + EPS) * g


def _sigmoid(x):
    return 1.0 / (1.0 + jnp.exp(-x))


def _dot(a, b):
    return jnp.dot(a, b, preferred_element_type=F32)


def _split3(x):
    hi = x.astype(BF16)
    r1 = x - hi.astype(F32)
    mid = r1.astype(BF16)
    lo = (r1 - mid.astype(F32)).astype(BF16)
    return hi, mid, lo


def _row_in_batch(tile_in_batch):
    return tile_in_batch * TM + lax.broadcasted_iota(jnp.int32, (TM, 1), 0)


def _const_spec(shape):
    return pl.BlockSpec(shape, lambda *_: (0,) * len(shape), pipeline_mode=pl.Buffered(1))


def _row_spec(width):
    return pl.BlockSpec((TM, width), lambda i: (i, 0))


def _params(sem):
    return pltpu.CompilerParams(dimension_semantics=sem, vmem_limit_bytes=VMEM_LIMIT)


def _ffn_body(h_ref, g_ref, wg_ref, wu_ref, wd_ref, o_ref):
    h = h_ref[...]
    x = _rms(h, g_ref[...]).astype(BF16)
    gate = _dot(x, wg_ref[...])
    up = _dot(x, wu_ref[...])
    a = (gate * _sigmoid(gate) * up).astype(BF16)
    o_ref[...] = h + 0.5 * _dot(a, wd_ref[...])


def _ffn(h, g, wg, wu, wd):
    return pl.pallas_call(
        _ffn_body,
        name="ffn",
        grid=(T // TM,),
        in_specs=[_row_spec(D_MODEL), _const_spec((1, D_MODEL)),
                  _const_spec((D_MODEL, D_FF)), _const_spec((D_MODEL, D_FF)),
                  _const_spec((D_FF, D_MODEL))],
        out_specs=_row_spec(D_MODEL),
        out_shape=jax.ShapeDtypeStruct((T, D_MODEL), F32),
        compiler_params=_params(("arbitrary",)),
    )(h, g.reshape(1, D_MODEL), wg, wu, wd)


def _mixa_body(h_ref, g_ref, win_ref, cw_ref, wout_ref, o_ref, buf_ref):
    t = pl.program_id(0) % NT

    @pl.when(t == 0)
    def _():
        buf_ref[0:SC_HALO, :] = jnp.zeros((SC_HALO, D_MODEL), F32)

    h = h_ref[...]
    u = _rms(h, g_ref[...]).astype(BF16)
    p = _dot(u, win_ref[...])
    b_gate = p[:, :D_MODEL]
    cv = p[:, D_MODEL:2 * D_MODEL] * p[:, 2 * D_MODEL:]
    cv = jnp.where(_row_in_batch(t) >= PAD, cv, 0.0)
    buf_ref[SC_HALO:SC_HALO + TM, :] = cv
    y = cw_ref[SC_WIDTH - 1:SC_WIDTH, :] * cv
    for k in range(SC_WIDTH - 1):
        shift = SC_WIDTH - 1 - k
        y = y + cw_ref[k:k + 1, :] * buf_ref[SC_HALO - shift:SC_HALO - shift + TM, :]
    z = (b_gate * y).astype(BF16)
    o_ref[...] = h + _dot(z, wout_ref[...])
    buf_ref[0:SC_HALO, :] = buf_ref[TM:TM + SC_HALO, :]


def _mixa(h, g, w_in, conv_w, w_out):
    return pl.pallas_call(
        _mixa_body,
        name="short_conv_mixer",
        grid=(T // TM,),
        in_specs=[_row_spec(D_MODEL), _const_spec((1, D_MODEL)),
                  _const_spec((D_MODEL, 3 * D_MODEL)), _const_spec((SC_WIDTH, D_MODEL)),
                  _const_spec((D_MODEL, D_MODEL))],
        out_specs=_row_spec(D_MODEL),
        out_shape=jax.ShapeDtypeStruct((T, D_MODEL), F32),
        scratch_shapes=[pltpu.VMEM((TM + SC_HALO, D_MODEL), F32)],
        compiler_params=_params(("arbitrary",)),
    )(h, g.reshape(1, D_MODEL), w_in, conv_w, w_out)


def _mixb_body(h_ref, g_ref, win_ref, cw_ref, cb_ref, lng_ref, lnb_ref, wout_ref,
               o_ref, buf_ref, y_ref):
    t = pl.program_id(0) % NT

    @pl.when(t == 0)
    def _():
        buf_ref[:, 0:CONF_HALO, :] = jnp.zeros((D_MODEL // LANES, CONF_HALO, LANES), F32)

    h = h_ref[...]
    u = _rms(h, g_ref[...]).astype(BF16)
    p = _dot(u, win_ref[...])
    glu = p[:, :D_MODEL] * _sigmoid(p[:, D_MODEL:])
    glu = jnp.where(_row_in_batch(t) >= PAD, glu, 0.0)
    n_col = D_MODEL // LANES
    for c in range(n_col):
        buf_ref[c, CONF_HALO:CONF_HALO + TM, :] = glu[:, c * LANES:(c + 1) * LANES]

    first = CONF_HALO - (CONF_WIDTH - 1)

    def lane_tile(c, carry):
        w = cw_ref[c]
        bias = jnp.broadcast_to(cb_ref[c], (CONF_ROWS, LANES))
        for r in range(0, TM, CONF_ROWS):
            acc = bias
            for k in range(CONF_WIDTH):
                acc = acc + w[k:k + 1, :] * buf_ref[c, r + first + k:r + first + k + CONF_ROWS, :]
            y_ref[c, r:r + CONF_ROWS, :] = acc
        buf_ref[c, 0:CONF_HALO, :] = buf_ref[c, TM:TM + CONF_HALO, :]
        return carry

    lax.fori_loop(0, n_col, lane_tile, 0)

    y = jnp.concatenate([y_ref[c] for c in range(n_col)], axis=1)
    mu = jnp.mean(y, axis=-1, keepdims=True)
    yc = y - mu
    var = jnp.mean(yc * yc, axis=-1, keepdims=True)
    ln = yc * lax.rsqrt(var + EPS) * lng_ref[...] + lnb_ref[...]
    s = (ln * _sigmoid(ln)).astype(BF16)
    o_ref[...] = h + _dot(s, wout_ref[...])


def _mixb(h, g, w_in, conv_w, conv_b, ln_g, ln_b, w_out):
    row = lambda v: v.reshape(1, D_MODEL)
    n_col = D_MODEL // LANES
    conv_w = conv_w.reshape(CONF_WIDTH, n_col, LANES).transpose(1, 0, 2)
    conv_b = conv_b.reshape(n_col, 1, LANES)
    return pl.pallas_call(
        _mixb_body,
        name="conformer_mixer",
        grid=(T // TM,),
        in_specs=[_row_spec(D_MODEL), _const_spec((1, D_MODEL)),
                  _const_spec((D_MODEL, 2 * D_MODEL)), _const_spec((n_col, CONF_WIDTH, LANES)),
                  _const_spec((n_col, 1, LANES)), _const_spec((1, D_MODEL)), _const_spec((1, D_MODEL)),
                  _const_spec((D_MODEL, D_MODEL))],
        out_specs=_row_spec(D_MODEL),
        out_shape=jax.ShapeDtypeStruct((T, D_MODEL), F32),
        scratch_shapes=[pltpu.VMEM((n_col, TM + CONF_HALO, LANES), F32),
                        pltpu.VMEM((n_col, TM, LANES), F32)],
        compiler_params=_params(("arbitrary",)),
    )(h, row(g), w_in, conv_w, conv_b, row(ln_g), row(ln_b), w_out)


def _head_lanes(lane, e):
    chan = (lane >= e * HEAD_DIM) & (lane < (e + 1) * HEAD_DIM)
    a = LANES + AUG_STRIDE * e
    return chan | ((lane >= a) & (lane < a + 2 * N_SPLIT))


def _fox_proj_body(h_ref, g_ref, wq_ref, wk_ref, wv_ref, wf_ref, bf_ref, gq_ref, gk_ref,
                   tri_ref, selq_ref, selk_ref, qx_ref, kx_ref, vx_ref, carry_ref):
    t = pl.program_id(0) % NT

    @pl.when(t == 0)
    def _():
        carry_ref[...] = jnp.zeros_like(carry_ref)

    u = _rms(h_ref[...], g_ref[...]).astype(BF16)
    real = _row_in_batch(t) >= PAD

    f_logit = _dot(u, wf_ref[...]) + bf_ref[...]
    log_f = jnp.minimum(f_logit, 0.0) - jnp.log1p(jnp.exp(-jnp.abs(f_logit)))
    log_f = jnp.where(real, log_f, 0.0)
    c3 = _dot(tri_ref[...], jnp.concatenate(_split3(log_f), axis=1))
    cum = (c3[:, :LANES] + c3[:, LANES:2 * LANES] + c3[:, 2 * LANES:]) + carry_ref[0:1, :]
    carry_ref[...] = jnp.broadcast_to(cum[TM - 1:TM, :], carry_ref.shape)

    lane = lax.broadcasted_iota(jnp.int32, (1, LANES), 1)
    sub = lane % AUG_STRIDE
    in_aug = lane < 2 * AUG_STRIDE
    ones_q = (in_aug & (sub >= N_SPLIT) & (sub < 2 * N_SPLIT)).astype(F32)
    ones_k = (in_aug & (sub < N_SPLIT)).astype(F32)
    ones_v = (lane == 0).astype(BF16)
    ones_v = jnp.broadcast_to(ones_v, (TM, LANES))

    neg_ck = jnp.where(real, -LOG2E * cum, MASK_VALUE)
    aug_q = _dot(jnp.concatenate(_split3(LOG2E * cum), axis=1), selq_ref[...])
    aug_k = _dot(jnp.concatenate(_split3(neg_ck), axis=1), selk_ref[...])

    first_head = lane < HEAD_DIM
    gq = gq_ref[...] * (LOG2E * HEAD_DIM ** -0.5)
    gk = gk_ref[...]

    def normed(x, gain):
        x2 = x * x
        ms0 = jnp.sum(jnp.where(first_head, x2, 0.0), axis=-1, keepdims=True)
        ms1 = jnp.sum(jnp.where(first_head, 0.0, x2), axis=-1, keepdims=True)
        ms = jnp.where(first_head, ms0, ms1) * (1.0 / HEAD_DIM)
        return x * lax.rsqrt(ms + EPS) * gain

    q = _dot(u, wq_ref[...])
    k = _dot(u, wk_ref[...])
    v = _dot(u, wv_ref[...])
    for j in range(N_PAIRS):
        src = slice(j * LANES, (j + 1) * LANES)
        chan = slice(j * PAIR_W, j * PAIR_W + LANES)
        bias = slice(j * PAIR_W + LANES, (j + 1) * PAIR_W)
        qx_ref[:, chan] = normed(q[:, src], gq).astype(BF16)
        qx_ref[:, bias] = (aug_q[:, src] + ones_q).astype(BF16)
        kx_ref[:, chan] = normed(k[:, src], gk).astype(BF16)
        kx_ref[:, bias] = (aug_k[:, src] + ones_k).astype(BF16)
        vx_ref[:, chan] = v[:, src].astype(BF16)
        vx_ref[:, bias] = ones_v


def _fox_proj(h, g, wq, wk, wv, wf, bf, gq, gk, tri, selq, selk):
    out = jax.ShapeDtypeStruct((T, D_X), BF16)
    w_spec = _const_spec((D_MODEL, D_MODEL))
    sel_spec = _const_spec((N_SPLIT * LANES, D_MODEL))
    return pl.pallas_call(
        _fox_proj_body,
        name="fox_proj",
        grid=(T // TM,),
        in_specs=[_row_spec(D_MODEL), _const_spec((1, D_MODEL)), w_spec, w_spec, w_spec,
                  _const_spec((D_MODEL, LANES)), _const_spec((1, LANES)),
                  _const_spec((1, LANES)), _const_spec((1, LANES)),
                  _const_spec((TM, TM)), sel_spec, sel_spec],
        out_specs=[_row_spec(D_X)] * 3,
        out_shape=[out] * 3,
        scratch_shapes=[pltpu.VMEM((8, LANES), F32)],
        compiler_params=_params(("arbitrary",)),
    )(h, g.reshape(1, D_MODEL), wq, wk, wv, wf, bf, gq, gk, tri, selq, selk)


def _fox_attn_body(q_ref, k_ref, v_ref, o_ref, s_ref, m_ref, l_ref, acc_ref):
    i = pl.program_id(2)
    lane_x = lax.broadcasted_iota(jnp.int32, (1, PAIR_W), 1)
    first_head = lax.broadcasted_iota(jnp.int32, (1, LANES), 1) < HEAD_DIM
    heads = [(g, e) for g in range(ATTN_PAIRS) for e in range(2)]
    q_heads = []
    for g, e in heads:
        qx = q_ref[0, :, g * PAIR_W:(g + 1) * PAIR_W]
        q_heads.append(jnp.where(_head_lanes(lane_x, e), qx, jnp.zeros_like(qx)))

    def span(kb, n, ends_on_diagonal=False):
        keys = pl.ds(pl.multiple_of(kb * TM, TM), n * TM)
        for hd, (g, e) in enumerate(heads):
            k = k_ref[0, keys, g * PAIR_W:(g + 1) * PAIR_W]
            s = lax.dot_general(q_heads[hd], k, (((1,), (1,)), ((), ())), preferred_element_type=F32)
            if ends_on_diagonal:
                row = lax.broadcasted_iota(jnp.int32, (TM, n * TM), 0)
                col = lax.broadcasted_iota(jnp.int32, (TM, n * TM), 1)
                s = jnp.where(col <= row + (n - 1) * TM, s, MASK_VALUE)
            s_ref[hd, :, :n * TM] = s
        for g in range(ATTN_PAIRS):
            v = v_ref[0, keys, g * PAIR_W:(g + 1) * PAIR_W]
            alpha, pv = [], []
            for e in range(2):
                hd = 2 * g + e
                s = s_ref[hd, :, :n * TM]
                m_old = m_ref[hd]
                m_new = jnp.maximum(m_old, jnp.max(s, axis=-1, keepdims=True))
                p = jnp.exp2(s - m_new).astype(BF16)
                m_ref[hd] = m_new
                alpha.append(jnp.exp2(m_old - m_new))
                pv.append(_dot(p, v))
                l_ref[hd] = alpha[e] * l_ref[hd] + pv[e][:, LANES:]
            acc_ref[g] = (jnp.where(first_head, alpha[0], alpha[1]) * acc_ref[g]
                          + jnp.where(first_head, pv[0][:, :LANES], pv[1][:, :LANES]))

    m_ref[...] = jnp.full(m_ref.shape, MASK_VALUE, F32)
    l_ref[...] = jnp.zeros(l_ref.shape, F32)
    acc_ref[...] = jnp.zeros(acc_ref.shape, F32)

    @pl.when(i % 2 == 0)
    def _():
        span(i, 1, ends_on_diagonal=True)

    @pl.when(i % 2 == 1)
    def _():
        span(i - 1, 2, ends_on_diagonal=True)

    def step(u, carry):
        span(2 * u, 2)
        return carry

    lax.fori_loop(0, i // 2, step, 0)
    for g in range(ATTN_PAIRS):
        denom = jnp.where(first_head, l_ref[2 * g, :, 0:1], l_ref[2 * g + 1, :, 0:1])
        o_ref[0, :, g * LANES:(g + 1) * LANES] = (acc_ref[g] * (1.0 / denom)).astype(BF16)


def _fox_attn(qx, kx, vx):
    shape3 = (BATCH, LP, D_X)
    n_heads = 2 * ATTN_PAIRS
    q_spec = pl.BlockSpec((1, TM, ATTN_PAIRS * PAIR_W), lambda b, p, i: (b, i, p))
    kv_spec = pl.BlockSpec((1, LP, ATTN_PAIRS * PAIR_W), lambda b, p, i: (b, 0, p),
                           pipeline_mode=pl.Buffered(1))
    return pl.pallas_call(
        _fox_attn_body,
        name="fox_attn",
        grid=(BATCH, N_PAIRS // ATTN_PAIRS, NT),
        in_specs=[q_spec, kv_spec, kv_spec],
        out_specs=pl.BlockSpec((1, TM, ATTN_PAIRS * LANES), lambda b, p, i: (b, i, p)),
        out_shape=jax.ShapeDtypeStruct((BATCH, LP, D_MODEL), BF16),
        scratch_shapes=[pltpu.VMEM((n_heads, TM, 2 * TM), F32),
                        pltpu.VMEM((n_heads, TM, 1), F32),
                        pltpu.VMEM((n_heads, TM, LANES), F32),
                        pltpu.VMEM((ATTN_PAIRS, TM, LANES), F32)],
        compiler_params=_params(("arbitrary", "arbitrary", "arbitrary")),
    )(qx.reshape(shape3), kx.reshape(shape3), vx.reshape(shape3)).reshape(T, D_MODEL)


def _fox_out_body(h_ref, o_ref, w_ref, out_ref):
    out_ref[...] = h_ref[...] + _dot(o_ref[...], w_ref[...])


def _fox_out(h, o, w_out):
    return pl.pallas_call(
        _fox_out_body,
        name="fox_out",
        grid=(T // TM,),
        in_specs=[_row_spec(D_MODEL), _row_spec(D_MODEL), _const_spec((D_MODEL, D_MODEL))],
        out_specs=_row_spec(D_MODEL),
        out_shape=jax.ShapeDtypeStruct((T, D_MODEL), F32),
        compiler_params=_params(("arbitrary",)),
    )(h, o, w_out)


def _bias_selector(first_lane):
    r = jnp.arange(N_SPLIT * LANES)
    piece, head = r // LANES, r % LANES
    col = (head // 2) * LANES + (head % 2) * AUG_STRIDE + first_lane + piece
    hit = (jnp.arange(N_PAIRS * LANES)[None, :] == col[:, None]) & (head < N_HEADS)[:, None]
    return hit.astype(BF16)


def _fox(h, g, w_in, b_f, q_g, k_g, w_out):
    wq = w_in[:, :D_MODEL].astype(BF16)
    wk = w_in[:, D_MODEL:2 * D_MODEL].astype(BF16)
    wv = w_in[:, 2 * D_MODEL:3 * D_MODEL].astype(BF16)
    wf = jnp.pad(w_in[:, 3 * D_MODEL:], ((0, 0), (0, LANES - N_HEADS))).astype(BF16)
    bf = jnp.pad(b_f, (0, LANES - N_HEADS)).reshape(1, LANES)
    gq = jnp.tile(q_g, 2).reshape(1, LANES)
    gk = jnp.tile(k_g, 2).reshape(1, LANES)
    tri = (jnp.arange(TM)[None, :] <= jnp.arange(TM)[:, None]).astype(BF16)
    qx, kx, vx = _fox_proj(h, g, wq, wk, wv, wf, bf, gq, gk, tri,
                           _bias_selector(0), _bias_selector(N_SPLIT))
    return _fox_out(h, _fox_attn(qx, kx, vx), w_out.astype(BF16))


def kernel(x, meta, ffn_norm, ffn_w_gate, ffn_w_up, ffn_w_down, mix_norm, a_w_in, a_conv, a_w_out, b_w_in, b_conv, b_conv_bias, b_ln_g, b_ln_b, b_w_out, c_w_in, c_b_f, c_q_norm, c_k_norm, c_w_out):
    bsz = x.shape[0]
    meta_b = jnp.broadcast_to(meta[None].astype(x.dtype), (bsz, N_META, D_MODEL))
    h = jnp.concatenate([jnp.zeros((bsz, PAD, D_MODEL), x.dtype), meta_b, x], axis=1)
    h = h.reshape(T, D_MODEL)
    for i in range(DEPTH):
        m, j = i % N_MIXERS, i // N_MIXERS
        h = _ffn(h, ffn_norm[i, 0], ffn_w_gate[i, 0].astype(BF16), ffn_w_up[i, 0].astype(BF16),
                 ffn_w_down[i, 0].astype(BF16))
        if m == 0:
            h = _mixa(h, mix_norm[i], a_w_in[j].astype(BF16), a_conv[j], a_w_out[j].astype(BF16))
        elif m == 1:
            h = _mixb(h, mix_norm[i], b_w_in[j].astype(BF16), b_conv[j], b_conv_bias[j],
                      b_ln_g[j], b_ln_b[j], b_w_out[j].astype(BF16))
        else:
            h = _fox(h, mix_norm[i], c_w_in[j], c_b_f[j], c_q_norm[j], c_k_norm[j], c_w_out[j])
        h = _ffn(h, ffn_norm[i, 1], ffn_w_gate[i, 1].astype(BF16), ffn_w_up[i, 1].astype(BF16),
                 ffn_w_down[i, 1].astype(BF16))
    return h.reshape(bsz, LP, D_MODEL)[:, PAD + N_META:]
```

```python
import jax
import jax.numpy as jnp
from jax import lax
from jax.experimental import pallas as pl
from jax.experimental.pallas import tpu as pltpu

D_MODEL = 1024
BATCH = 2
SEQ = 8192
DEPTH = 4
N_MIXERS = 3
N_META = 16
D_FF = 2816
SC_WIDTH = 3
CONF_WIDTH = 31
N_HEADS = 16
HEAD_DIM = D_MODEL // N_HEADS
EPS = 1e-6
MASK_VALUE = -1e30
LOG2E = 1.4426950408889634

LANES = 128
PAD = (-(N_META + SEQ)) % LANES
LP = PAD + N_META + SEQ
T = BATCH * LP
FRONT = PAD + N_META
TM = 640
NT = LP // TM
TM_LAST = 512
N_PAIRS = N_HEADS // 2
PAIR_W = 2 * LANES
D_X = N_PAIRS * PAIR_W
N_SPLIT = 3
AUG_STRIDE = 8
ATTN_PAIRS = 2
SC_HALO = 8
CONF_HALO = 32
CONF_ROWS = 64
VMEM_LIMIT = 56 * 1024 * 1024

F32 = jnp.float32
BF16 = jnp.bfloat16


def _rms(x, g):
    return x * lax.rsqrt(jnp.mean(x * x, axis=-1, keepdims=True) + EPS) * g


def _sigmoid(x):
    return 1.0 / (1.0 + jnp.exp(-x))


def _dot(a, b):
    return jnp.dot(a, b, preferred_element_type=F32)


def _split3(x):
    hi = x.astype(BF16)
    r1 = x - hi.astype(F32)
    mid = r1.astype(BF16)
    lo = (r1 - mid.astype(F32)).astype(BF16)
    return hi, mid, lo


def _row_in_batch(tile_in_batch):
    return tile_in_batch * TM + lax.broadcasted_iota(jnp.int32, (TM, 1), 0)


def _const_spec(shape):
    return pl.BlockSpec(shape, lambda *_: (0,) * len(shape), pipeline_mode=pl.Buffered(1))


def _row_spec(width):
    return pl.BlockSpec((TM, width), lambda i: (i, 0))


def _params(sem):
    return pltpu.CompilerParams(dimension_semantics=sem, vmem_limit_bytes=VMEM_LIMIT)


def _ffn_rows(h, g_ref, wg_ref, wu_ref, wd_ref):
    x = _rms(h, g_ref[...]).astype(BF16)
    gate = _dot(x, wg_ref[...])
    up = _dot(x, wu_ref[...])
    a = (gate * _sigmoid(gate) * up).astype(BF16)
    return h + 0.5 * _dot(a, wd_ref[...])


def _ffn_body(h_ref, g_ref, wg_ref, wu_ref, wd_ref, o_ref):
    o_ref[...] = _ffn_rows(h_ref[...], g_ref, wg_ref, wu_ref, wd_ref)


def _ffn_first_body(x_ref, front_ref, g_ref, wg_ref, wu_ref, wd_ref, o_ref):
    xb = x_ref[...]
    head = jnp.concatenate([front_ref[...], xb[:TM - FRONT]], axis=0)
    h = jnp.where(pl.program_id(1) == 0, head, xb)
    o_ref[...] = _ffn_rows(h, g_ref, wg_ref, wu_ref, wd_ref)


_FFN_WEIGHT_SPECS = [_const_spec((1, D_MODEL)), _const_spec((D_MODEL, D_FF)),
                     _const_spec((D_MODEL, D_FF)), _const_spec((D_FF, D_MODEL))]


def _ffn(h, g, wg, wu, wd):
    return pl.pallas_call(
        _ffn_body,
        name="ffn",
        grid=(T // TM,),
        in_specs=[_row_spec(D_MODEL)] + _FFN_WEIGHT_SPECS,
        out_specs=_row_spec(D_MODEL),
        out_shape=jax.ShapeDtypeStruct((T, D_MODEL), F32),
        compiler_params=_params(("arbitrary",)),
    )(h, g.reshape(1, D_MODEL), wg, wu, wd)


def _ffn_first(x, front, g, wg, wu, wd):
    x_spec = pl.BlockSpec(
        (pl.Element(TM), pl.Element(D_MODEL)),
        lambda b, t: (pl.multiple_of(b * SEQ + jnp.maximum(t * TM - FRONT, 0), LANES), 0))
    return pl.pallas_call(
        _ffn_first_body,
        name="ffn_first",
        grid=(BATCH, NT),
        in_specs=[x_spec, _const_spec((FRONT, D_MODEL))] + _FFN_WEIGHT_SPECS,
        out_specs=pl.BlockSpec((TM, D_MODEL), lambda b, t: (b * NT + t, 0)),
        out_shape=jax.ShapeDtypeStruct((T, D_MODEL), F32),
        compiler_params=_params(("arbitrary", "arbitrary")),
    )(x.reshape(BATCH * SEQ, D_MODEL), front, g.reshape(1, D_MODEL), wg, wu, wd)


def _ffn_last(h, g, wg, wu, wd):
    n_tiles = SEQ // TM_LAST
    h_spec = pl.BlockSpec(
        (pl.Element(TM_LAST), pl.Element(D_MODEL)),
        lambda b, t: (pl.multiple_of(b * LP + FRONT + t * TM_LAST, LANES), 0))
    return pl.pallas_call(
        _ffn_body,
        name="ffn_last",
        grid=(BATCH, n_tiles),
        in_specs=[h_spec] + _FFN_WEIGHT_SPECS,
        out_specs=pl.BlockSpec((TM_LAST, D_MODEL), lambda b, t: (b * n_tiles + t, 0)),
        out_shape=jax.ShapeDtypeStruct((BATCH * SEQ, D_MODEL), F32),
        compiler_params=_params(("arbitrary", "arbitrary")),
    )(h, g.reshape(1, D_MODEL), wg, wu, wd).reshape(BATCH, SEQ, D_MODEL)


def _mixa_body(h_ref, g_ref, win_ref, cw_ref, wout_ref, o_ref, buf_ref):
    t = pl.program_id(0) % NT

    @pl.when(t == 0)
    def _():
        buf_ref[0:SC_HALO, :] = jnp.zeros((SC_HALO, D_MODEL), F32)

    h = h_ref[...]
    u = _rms(h, g_ref[...]).astype(BF16)
    p = _dot(u, win_ref[...])
    b_gate = p[:, :D_MODEL]
    cv = p[:, D_MODEL:2 * D_MODEL] * p[:, 2 * D_MODEL:]
    cv = jnp.where(_row_in_batch(t) >= PAD, cv, 0.0)
    buf_ref[SC_HALO:SC_HALO + TM, :] = cv
    y = cw_ref[SC_WIDTH - 1:SC_WIDTH, :] * cv
    for k in range(SC_WIDTH - 1):
        shift = SC_WIDTH - 1 - k
        y = y + cw_ref[k:k + 1, :] * buf_ref[SC_HALO - shift:SC_HALO - shift + TM, :]
    z = (b_gate * y).astype(BF16)
    o_ref[...] = h + _dot(z, wout_ref[...])
    buf_ref[0:SC_HALO, :] = buf_ref[TM:TM + SC_HALO, :]


def _mixa(h, g, w_in, conv_w, w_out):
    return pl.pallas_call(
        _mixa_body,
        name="short_conv_mixer",
        grid=(T // TM,),
        in_specs=[_row_spec(D_MODEL), _const_spec((1, D_MODEL)),
                  _const_spec((D_MODEL, 3 * D_MODEL)), _const_spec((SC_WIDTH, D_MODEL)),
                  _const_spec((D_MODEL, D_MODEL))],
        out_specs=_row_spec(D_MODEL),
        out_shape=jax.ShapeDtypeStruct((T, D_MODEL), F32),
        scratch_shapes=[pltpu.VMEM((TM + SC_HALO, D_MODEL), F32)],
        compiler_params=_params(("arbitrary",)),
    )(h, g.reshape(1, D_MODEL), w_in, conv_w, w_out)


def _mixb_body(h_ref, g_ref, win_ref, cw_ref, cb_ref, lng_ref, lnb_ref, wout_ref,
               o_ref, buf_ref, y_ref):
    t = pl.program_id(0) % NT

    @pl.when(t == 0)
    def _():
        buf_ref[:, 0:CONF_HALO, :] = jnp.zeros((D_MODEL // LANES, CONF_HALO, LANES), F32)

    h = h_ref[...]
    u = _rms(h, g_ref[...]).astype(BF16)
    p = _dot(u, win_ref[...])
    glu = p[:, :D_MODEL] * _sigmoid(p[:, D_MODEL:])
    glu = jnp.where(_row_in_batch(t) >= PAD, glu, 0.0)
    n_col = D_MODEL // LANES
    for c in range(n_col):
        buf_ref[c, CONF_HALO:CONF_HALO + TM, :] = glu[:, c * LANES:(c + 1) * LANES]

    first = CONF_HALO - (CONF_WIDTH - 1)

    def lane_tile(c, carry):
        w = cw_ref[c]
        bias = jnp.broadcast_to(cb_ref[c], (CONF_ROWS, LANES))
        for r in range(0, TM, CONF_ROWS):
            acc = bias
            for k in range(CONF_WIDTH):
                acc = acc + w[k:k + 1, :] * buf_ref[c, r + first + k:r + first + k + CONF_ROWS, :]
            y_ref[c, r:r + CONF_ROWS, :] = acc
        buf_ref[c, 0:CONF_HALO, :] = buf_ref[c, TM:TM + CONF_HALO, :]
        return carry

    lax.fori_loop(0, n_col, lane_tile, 0)

    y = jnp.concatenate([y_ref[c] for c in range(n_col)], axis=1)
    mu = jnp.mean(y, axis=-1, keepdims=True)
    yc = y - mu
    var = jnp.mean(yc * yc, axis=-1, keepdims=True)
    ln = yc * lax.rsqrt(var + EPS) * lng_ref[...] + lnb_ref[...]
    s = (ln * _sigmoid(ln)).astype(BF16)
    o_ref[...] = h + _dot(s, wout_ref[...])


def _mixb(h, g, w_in, conv_w, conv_b, ln_g, ln_b, w_out):
    row = lambda v: v.reshape(1, D_MODEL)
    n_col = D_MODEL // LANES
    conv_w = conv_w.reshape(CONF_WIDTH, n_col, LANES).transpose(1, 0, 2)
    conv_b = conv_b.reshape(n_col, 1, LANES)
    return pl.pallas_call(
        _mixb_body,
        name="conformer_mixer",
        grid=(T // TM,),
        in_specs=[_row_spec(D_MODEL), _const_spec((1, D_MODEL)),
                  _const_spec((D_MODEL, 2 * D_MODEL)), _const_spec((n_col, CONF_WIDTH, LANES)),
                  _const_spec((n_col, 1, LANES)), _const_spec((1, D_MODEL)), _const_spec((1, D_MODEL)),
                  _const_spec((D_MODEL, D_MODEL))],
        out_specs=_row_spec(D_MODEL),
        out_shape=jax.ShapeDtypeStruct((T, D_MODEL), F32),
        scratch_shapes=[pltpu.VMEM((n_col, TM + CONF_HALO, LANES), F32),
                        pltpu.VMEM((n_col, TM, LANES), F32)],
        compiler_params=_params(("arbitrary",)),
    )(h, row(g), w_in, conv_w, conv_b, row(ln_g), row(ln_b), w_out)


def _head_lanes(lane, e):
    chan = (lane >= e * HEAD_DIM) & (lane < (e + 1) * HEAD_DIM)
    a = LANES + AUG_STRIDE * e
    return chan | ((lane >= a) & (lane < a + 2 * N_SPLIT))


def _fox_proj_body(h_ref, g_ref, wq_ref, wk_ref, wv_ref, wf_ref, bf_ref, gq_ref, gk_ref,
                   tri_ref, selq_ref, selk_ref, qx_ref, kx_ref, vx_ref, carry_ref):
    t = pl.program_id(0) % NT

    @pl.when(t == 0)
    def _():
        carry_ref[...] = jnp.zeros_like(carry_ref)

    u = _rms(h_ref[...], g_ref[...]).astype(BF16)
    real = _row_in_batch(t) >= PAD

    f_logit = _dot(u, wf_ref[...]) + bf_ref[...]
    log_f = jnp.minimum(f_logit, 0.0) - jnp.log1p(jnp.exp(-jnp.abs(f_logit)))
    log_f = jnp.where(real, log_f, 0.0)
    c3 = _dot(tri_ref[...], jnp.concatenate(_split3(log_f), axis=1))
    cum = (c3[:, :LANES] + c3[:, LANES:2 * LANES] + c3[:, 2 * LANES:]) + carry_ref[0:1, :]
    carry_ref[...] = jnp.broadcast_to(cum[TM - 1:TM, :], carry_ref.shape)

    lane = lax.broadcasted_iota(jnp.int32, (1, LANES), 1)
    sub = lane % AUG_STRIDE
    in_aug = lane < 2 * AUG_STRIDE
    ones_q = (in_aug & (sub >= N_SPLIT) & (sub < 2 * N_SPLIT)).astype(F32)
    ones_k = (in_aug & (sub < N_SPLIT)).astype(F32)
    ones_v = (lane == 0).astype(BF16)
    ones_v = jnp.broadcast_to(ones_v, (TM, LANES))

    neg_ck = jnp.where(real, -LOG2E * cum, MASK_VALUE)
    aug_q = _dot(jnp.concatenate(_split3(LOG2E * cum), axis=1), selq_ref[...])
    aug_k = _dot(jnp.concatenate(_split3(neg_ck), axis=1), selk_ref[...])

    first_head = lane < HEAD_DIM
    gq = gq_ref[...] * (LOG2E * HEAD_DIM ** -0.5)
    gk = gk_ref[...]

    def normed(x, gain):
        x2 = x * x
        ms0 = jnp.sum(jnp.where(first_head, x2, 0.0), axis=-1, keepdims=True)
        ms1 = jnp.sum(jnp.where(first_head, 0.0, x2), axis=-1, keepdims=True)
        ms = jnp.where(first_head, ms0, ms1) * (1.0 / HEAD_DIM)
        return x * lax.rsqrt(ms + EPS) * gain

    q = _dot(u, wq_ref[...])
    k = _dot(u, wk_ref[...])
    v = _dot(u, wv_ref[...])
    for j in range(N_PAIRS):
        src = slice(j * LANES, (j + 1) * LANES)
        chan = slice(j * PAIR_W, j * PAIR_W + LANES)
        bias = slice(j * PAIR_W + LANES, (j + 1) * PAIR_W)
        qx_ref[:, chan] = normed(q[:, src], gq).astype(BF16)
        qx_ref[:, bias] = (aug_q[:, src] + ones_q).astype(BF16)
        kx_ref[:, chan] = normed(k[:, src], gk).astype(BF16)
        kx_ref[:, bias] = (aug_k[:, src] + ones_k).astype(BF16)
        vx_ref[:, chan] = v[:, src].astype(BF16)
        vx_ref[:, bias] = ones_v


def _fox_proj(h, g, wq, wk, wv, wf, bf, gq, gk, tri, selq, selk):
    out = jax.ShapeDtypeStruct((T, D_X), BF16)
    w_spec = _const_spec((D_MODEL, D_MODEL))
    sel_spec = _const_spec((N_SPLIT * LANES, D_MODEL))
    return pl.pallas_call(
        _fox_proj_body,
        name="fox_proj",
        grid=(T // TM,),
        in_specs=[_row_spec(D_MODEL), _const_spec((1, D_MODEL)), w_spec, w_spec, w_spec,
                  _const_spec((D_MODEL, LANES)), _const_spec((1, LANES)),
                  _const_spec((1, LANES)), _const_spec((1, LANES)),
                  _const_spec((TM, TM)), sel_spec, sel_spec],
        out_specs=[_row_spec(D_X)] * 3,
        out_shape=[out] * 3,
        scratch_shapes=[pltpu.VMEM((8, LANES), F32)],
        compiler_params=_params(("arbitrary",)),
    )(h, g.reshape(1, D_MODEL), wq, wk, wv, wf, bf, gq, gk, tri, selq, selk)


def _fox_attn_body(q_ref, k_ref, v_ref, o_ref, s_ref, m_ref, l_ref, acc_ref):
    i = pl.program_id(2)
    lane_x = lax.broadcasted_iota(jnp.int32, (1, PAIR_W), 1)
    first_head = lax.broadcasted_iota(jnp.int32, (1, LANES), 1) < HEAD_DIM
    heads = [(g, e) for g in range(ATTN_PAIRS) for e in range(2)]
    q_heads = []
    for g, e in heads:
        qx = q_ref[0, :, g * PAIR_W:(g + 1) * PAIR_W]
        q_heads.append(jnp.where(_head_lanes(lane_x, e), qx, jnp.zeros_like(qx)))

    def span(kb, n, ends_on_diagonal=False):
        keys = pl.ds(pl.multiple_of(kb * TM, TM), n * TM)
        for hd, (g, e) in enumerate(heads):
            k = k_ref[0, keys, g * PAIR_W:(g + 1) * PAIR_W]
            s = lax.dot_general(q_heads[hd], k, (((1,), (1,)), ((), ())), preferred_element_type=F32)
            if ends_on_diagonal:
                row = lax.broadcasted_iota(jnp.int32, (TM, n * TM), 0)
                col = lax.broadcasted_iota(jnp.int32, (TM, n * TM), 1)
                s = jnp.where(col <= row + (n - 1) * TM, s, MASK_VALUE)
            s_ref[hd, :, :n * TM] = s
        for g in range(ATTN_PAIRS):
            v = v_ref[0, keys, g * PAIR_W:(g + 1) * PAIR_W]
            alpha, pv = [], []
            for e in range(2):
                hd = 2 * g + e
                s = s_ref[hd, :, :n * TM]
                m_old = m_ref[hd]
                m_new = jnp.maximum(m_old, jnp.max(s, axis=-1, keepdims=True))
                p = jnp.exp2(s - m_new).astype(BF16)
                m_ref[hd] = m_new
                alpha.append(jnp.exp2(m_old - m_new))
                pv.append(_dot(p, v))
                l_ref[hd] = alpha[e] * l_ref[hd] + pv[e][:, LANES:]
            acc_ref[g] = (jnp.where(first_head, alpha[0], alpha[1]) * acc_ref[g]
                          + jnp.where(first_head, pv[0][:, :LANES], pv[1][:, :LANES]))

    m_ref[...] = jnp.full(m_ref.shape, MASK_VALUE, F32)
    l_ref[...] = jnp.zeros(l_ref.shape, F32)
    acc_ref[...] = jnp.zeros(acc_ref.shape, F32)

    @pl.when(i % 2 == 0)
    def _():
        span(i, 1, ends_on_diagonal=True)

    @pl.when(i % 2 == 1)
    def _():
        span(i - 1, 2, ends_on_diagonal=True)

    def step(u, carry):
        span(2 * u, 2)
        return carry

    lax.fori_loop(0, i // 2, step, 0)
    for g in range(ATTN_PAIRS):
        denom = jnp.where(first_head, l_ref[2 * g, :, 0:1], l_ref[2 * g + 1, :, 0:1])
        o_ref[0, :, g * LANES:(g + 1) * LANES] = (acc_ref[g] * (1.0 / denom)).astype(BF16)


def _fox_attn(qx, kx, vx):
    shape3 = (BATCH, LP, D_X)
    n_heads = 2 * ATTN_PAIRS
    q_spec = pl.BlockSpec((1, TM, ATTN_PAIRS * PAIR_W), lambda b, p, i: (b, i, p))
    kv_spec = pl.BlockSpec((1, LP, ATTN_PAIRS * PAIR_W), lambda b, p, i: (b, 0, p),
                           pipeline_mode=pl.Buffered(1))
    return pl.pallas_call(
        _fox_attn_body,
        name="fox_attn",
        grid=(BATCH, N_PAIRS // ATTN_PAIRS, NT),
        in_specs=[q_spec, kv_spec, kv_spec],
        out_specs=pl.BlockSpec((1, TM, ATTN_PAIRS * LANES), lambda b, p, i: (b, i, p)),
        out_shape=jax.ShapeDtypeStruct((BATCH, LP, D_MODEL), BF16),
        scratch_shapes=[pltpu.VMEM((n_heads, TM, 2 * TM), F32),
                        pltpu.VMEM((n_heads, TM, 1), F32),
                        pltpu.VMEM((n_heads, TM, LANES), F32),
                        pltpu.VMEM((ATTN_PAIRS, TM, LANES), F32)],
        compiler_params=_params(("arbitrary", "arbitrary", "arbitrary")),
    )(qx.reshape(shape3), kx.reshape(shape3), vx.reshape(shape3)).reshape(T, D_MODEL)


def _fox_out_body(h_ref, o_ref, w_ref, out_ref):
    out_ref[...] = h_ref[...] + _dot(o_ref[...], w_ref[...])


def _fox_out(h, o, w_out):
    return pl.pallas_call(
        _fox_out_body,
        name="fox_out",
        grid=(T // TM,),
        in_specs=[_row_spec(D_MODEL), _row_spec(D_MODEL), _const_spec((D_MODEL, D_MODEL))],
        out_specs=_row_spec(D_MODEL),
        out_shape=jax.ShapeDtypeStruct((T, D_MODEL), F32),
        compiler_params=_params(("arbitrary",)),
    )(h, o, w_out)


def _bias_selector(first_lane):
    r = jnp.arange(N_SPLIT * LANES)
    piece, head = r // LANES, r % LANES
    col = (head // 2) * LANES + (head % 2) * AUG_STRIDE + first_lane + piece
    hit = (jnp.arange(N_PAIRS * LANES)[None, :] == col[:, None]) & (head < N_HEADS)[:, None]
    return hit.astype(BF16)


def _fox(h, g, w_in, b_f, q_g, k_g, w_out):
    wq = w_in[:, :D_MODEL].astype(BF16)
    wk = w_in[:, D_MODEL:2 * D_MODEL].astype(BF16)
    wv = w_in[:, 2 * D_MODEL:3 * D_MODEL].astype(BF16)
    wf = jnp.pad(w_in[:, 3 * D_MODEL:], ((0, 0), (0, LANES - N_HEADS))).astype(BF16)
    bf = jnp.pad(b_f, (0, LANES - N_HEADS)).reshape(1, LANES)
    gq = jnp.tile(q_g, 2).reshape(1, LANES)
    gk = jnp.tile(k_g, 2).reshape(1, LANES)
    tri = (jnp.arange(TM)[None, :] <= jnp.arange(TM)[:, None]).astype(BF16)
    qx, kx, vx = _fox_proj(h, g, wq, wk, wv, wf, bf, gq, gk, tri,
                           _bias_selector(0), _bias_selector(N_SPLIT))
    return _fox_out(h, _fox_attn(qx, kx, vx), w_out.astype(BF16))


def kernel(x, meta, ffn_norm, ffn_w_gate, ffn_w_up, ffn_w_down, mix_norm, a_w_in, a_conv, a_w_out, b_w_in, b_conv, b_conv_bias, b_ln_g, b_ln_b, b_w_out, c_w_in, c_b_f, c_q_norm, c_k_norm, c_w_out):
    front = jnp.concatenate([jnp.zeros((PAD, D_MODEL), x.dtype), meta.astype(x.dtype)], axis=0)
    h = None
    for i in range(DEPTH):
        m, j = i % N_MIXERS, i // N_MIXERS
        ffn_a = (ffn_norm[i, 0], ffn_w_gate[i, 0].astype(BF16), ffn_w_up[i, 0].astype(BF16),
                 ffn_w_down[i, 0].astype(BF16))
        h = _ffn_first(x, front, *ffn_a) if i == 0 else _ffn(h, *ffn_a)
        if m == 0:
            h = _mixa(h, mix_norm[i], a_w_in[j].astype(BF16), a_conv[j], a_w_out[j].astype(BF16))
        elif m == 1:
            h = _mixb(h, mix_norm[i], b_w_in[j].astype(BF16), b_conv[j], b_conv_bias[j],
                      b_ln_g[j], b_ln_b[j], b_w_out[j].astype(BF16))
        else:
            h = _fox(h, mix_norm[i], c_w_in[j], c_b_f[j], c_q_norm[j], c_k_norm[j], c_w_out[j])
        ffn_b = (ffn_norm[i, 1], ffn_w_gate[i, 1].astype(BF16), ffn_w_up[i, 1].astype(BF16),
                 ffn_w_down[i, 1].astype(BF16))
        h = _ffn_last(h, *ffn_b) if i == DEPTH - 1 else _ffn(h, *ffn_b)
    return h
```

```python
import jax
import jax.numpy as jnp
from jax import lax
from jax.experimental import pallas as pl
from jax.experimental.pallas import tpu as pltpu

D_MODEL = 1024
BATCH = 2
SEQ = 8192
DEPTH = 4
N_MIXERS = 3
N_META = 16
D_FF = 2816
SC_WIDTH = 3
CONF_WIDTH = 31
N_HEADS = 16
HEAD_DIM = D_MODEL // N_HEADS
EPS = 1e-6
MASK_VALUE = -1e30
LOG2E = 1.4426950408889634

LANES = 128
PAD = (-(N_META + SEQ)) % LANES
LP = PAD + N_META + SEQ
T = BATCH * LP
FRONT = PAD + N_META
TM = 640
NT = LP // TM
TM_LAST = 512
N_PAIRS = N_HEADS // 2
PAIR_W = 2 * LANES
D_X = N_PAIRS * PAIR_W
N_SPLIT = 3
AUG_STRIDE = 8
ATTN_PAIRS = 2
SC_HALO = 8
CONF_HALO = 32
CONF_ROWS = 64
VMEM_LIMIT = 56 * 1024 * 1024

F32 = jnp.float32
BF16 = jnp.bfloat16


def _rms(x, g):
    return x * lax.rsqrt(jnp.mean(x * x, axis=-1, keepdims=True) + EPS) * g


def _sigmoid(x):
    return 1.0 / (1.0 + jnp.exp(-x))


def _dot(a, b):
    return jnp.dot(a, b, preferred_element_type=F32)


def _split3(x):
    hi = x.astype(BF16)
    r1 = x - hi.astype(F32)
    mid = r1.astype(BF16)
    lo = (r1 - mid.astype(F32)).astype(BF16)
    return hi, mid, lo


def _row_in_batch(tile_in_batch):
    return tile_in_batch * TM + lax.broadcasted_iota(jnp.int32, (TM, 1), 0)


def _const_spec(shape):
    return pl.BlockSpec(shape, lambda *_: (0,) * len(shape), pipeline_mode=pl.Buffered(1))


def _picked_spec(pick, tail):
    return pl.BlockSpec((None,) * len(pick) + tail, lambda *_: tuple(pick) + (0,) * len(tail),
                        pipeline_mode=pl.Buffered(1))


def _row_spec(width):
    return pl.BlockSpec((TM, width), lambda i: (i, 0))


def _params(sem):
    return pltpu.CompilerParams(dimension_semantics=sem, vmem_limit_bytes=VMEM_LIMIT)


def _ffn_rows(h, g_ref, wg_ref, wu_ref, wd_ref):
    x = _rms(h, g_ref[...]).astype(BF16)
    gate = _dot(x, wg_ref[...])
    up = _dot(x, wu_ref[...])
    a = (gate * _sigmoid(gate) * up).astype(BF16)
    return h + 0.5 * _dot(a, wd_ref[...])


def _ffn_body(h_ref, g_ref, wg_ref, wu_ref, wd_ref, o_ref):
    o_ref[...] = _ffn_rows(h_ref[...], g_ref, wg_ref, wu_ref, wd_ref)


def _ffn_first_body(x_ref, front_ref, g_ref, wg_ref, wu_ref, wd_ref, o_ref):
    xb = x_ref[...]
    head = jnp.concatenate([front_ref[...], xb[:TM - FRONT]], axis=0)
    h = jnp.where(pl.program_id(1) == 0, head, xb)
    o_ref[...] = _ffn_rows(h, g_ref, wg_ref, wu_ref, wd_ref)


def _ffn_weight_specs(pick):
    return [_picked_spec(pick, (1, D_MODEL)), _picked_spec(pick, (D_MODEL, D_FF)),
            _picked_spec(pick, (D_MODEL, D_FF)), _picked_spec(pick, (D_FF, D_MODEL))]


def _ffn(h, pick, g, wg, wu, wd):
    return pl.pallas_call(
        _ffn_body,
        name="ffn",
        grid=(T // TM,),
        in_specs=[_row_spec(D_MODEL)] + _ffn_weight_specs(pick),
        out_specs=_row_spec(D_MODEL),
        out_shape=jax.ShapeDtypeStruct((T, D_MODEL), F32),
        compiler_params=_params(("arbitrary",)),
    )(h, g, wg, wu, wd)


def _ffn_first(x, front, pick, g, wg, wu, wd):
    x_spec = pl.BlockSpec(
        (pl.Element(TM), pl.Element(D_MODEL)),
        lambda b, t: (pl.multiple_of(b * SEQ + jnp.maximum(t * TM - FRONT, 0), LANES), 0))
    return pl.pallas_call(
        _ffn_first_body,
        name="ffn_first",
        grid=(BATCH, NT),
        in_specs=[x_spec, _const_spec((FRONT, D_MODEL))] + _ffn_weight_specs(pick),
        out_specs=pl.BlockSpec((TM, D_MODEL), lambda b, t: (b * NT + t, 0)),
        out_shape=jax.ShapeDtypeStruct((T, D_MODEL), F32),
        compiler_params=_params(("arbitrary", "arbitrary")),
    )(x.reshape(BATCH * SEQ, D_MODEL), front, g, wg, wu, wd)


def _ffn_last(h, pick, g, wg, wu, wd):
    n_tiles = SEQ // TM_LAST
    h_spec = pl.BlockSpec(
        (pl.Element(TM_LAST), pl.Element(D_MODEL)),
        lambda b, t: (pl.multiple_of(b * LP + FRONT + t * TM_LAST, LANES), 0))
    return pl.pallas_call(
        _ffn_body,
        name="ffn_last",
        grid=(BATCH, n_tiles),
        in_specs=[h_spec] + _ffn_weight_specs(pick),
        out_specs=pl.BlockSpec((TM_LAST, D_MODEL), lambda b, t: (b * n_tiles + t, 0)),
        out_shape=jax.ShapeDtypeStruct((BATCH * SEQ, D_MODEL), F32),
        compiler_params=_params(("arbitrary", "arbitrary")),
    )(h, g, wg, wu, wd).reshape(BATCH, SEQ, D_MODEL)


def _mixa_body(h_ref, g_ref, win_ref, cw_ref, wout_ref, o_ref, buf_ref):
    t = pl.program_id(0) % NT

    @pl.when(t == 0)
    def _():
        buf_ref[0:SC_HALO, :] = jnp.zeros((SC_HALO, D_MODEL), F32)

    h = h_ref[...]
    u = _rms(h, g_ref[...]).astype(BF16)
    p = _dot(u, win_ref[...])
    b_gate = p[:, :D_MODEL]
    cv = p[:, D_MODEL:2 * D_MODEL] * p[:, 2 * D_MODEL:]
    cv = jnp.where(_row_in_batch(t) >= PAD, cv, 0.0)
    buf_ref[SC_HALO:SC_HALO + TM, :] = cv
    y = cw_ref[SC_WIDTH - 1:SC_WIDTH, :] * cv
    for k in range(SC_WIDTH - 1):
        shift = SC_WIDTH - 1 - k
        y = y + cw_ref[k:k + 1, :] * buf_ref[SC_HALO - shift:SC_HALO - shift + TM, :]
    z = (b_gate * y).astype(BF16)
    o_ref[...] = h + _dot(z, wout_ref[...])
    buf_ref[0:SC_HALO, :] = buf_ref[TM:TM + SC_HALO, :]


def _mixa(h, layer, j, g, w_in, conv_w, w_out):
    return pl.pallas_call(
        _mixa_body,
        name="short_conv_mixer",
        grid=(T // TM,),
        in_specs=[_row_spec(D_MODEL), _picked_spec((layer,), (1, D_MODEL)),
                  _picked_spec((j,), (D_MODEL, 3 * D_MODEL)), _picked_spec((j,), (SC_WIDTH, D_MODEL)),
                  _picked_spec((j,), (D_MODEL, D_MODEL))],
        out_specs=_row_spec(D_MODEL),
        out_shape=jax.ShapeDtypeStruct((T, D_MODEL), F32),
        scratch_shapes=[pltpu.VMEM((TM + SC_HALO, D_MODEL), F32)],
        compiler_params=_params(("arbitrary",)),
    )(h, g, w_in, conv_w, w_out)


def _mixb_body(h_ref, g_ref, win_ref, cw_ref, cb_ref, lng_ref, lnb_ref, wout_ref,
               o_ref, buf_ref, y_ref):
    t = pl.program_id(0) % NT

    @pl.when(t == 0)
    def _():
        buf_ref[:, 0:CONF_HALO, :] = jnp.zeros((D_MODEL // LANES, CONF_HALO, LANES), F32)

    h = h_ref[...]
    u = _rms(h, g_ref[...]).astype(BF16)
    p = _dot(u, win_ref[...])
    glu = p[:, :D_MODEL] * _sigmoid(p[:, D_MODEL:])
    glu = jnp.where(_row_in_batch(t) >= PAD, glu, 0.0)
    n_col = D_MODEL // LANES
    for c in range(n_col):
        buf_ref[c, CONF_HALO:CONF_HALO + TM, :] = glu[:, c * LANES:(c + 1) * LANES]

    first = CONF_HALO - (CONF_WIDTH - 1)

    def lane_tile(c, carry):
        w = cw_ref[c]
        bias = jnp.broadcast_to(cb_ref[c], (CONF_ROWS, LANES))
        for r in range(0, TM, CONF_ROWS):
            acc = bias
            for k in range(CONF_WIDTH):
                acc = acc + w[k:k + 1, :] * buf_ref[c, r + first + k:r + first + k + CONF_ROWS, :]
            y_ref[c, r:r + CONF_ROWS, :] = acc
        buf_ref[c, 0:CONF_HALO, :] = buf_ref[c, TM:TM + CONF_HALO, :]
        return carry

    lax.fori_loop(0, n_col, lane_tile, 0)

    y = jnp.concatenate([y_ref[c] for c in range(n_col)], axis=1)
    mu = jnp.mean(y, axis=-1, keepdims=True)
    yc = y - mu
    var = jnp.mean(yc * yc, axis=-1, keepdims=True)
    ln = yc * lax.rsqrt(var + EPS) * lng_ref[...] + lnb_ref[...]
    s = (ln * _sigmoid(ln)).astype(BF16)
    o_ref[...] = h + _dot(s, wout_ref[...])


def _mixb(h, g, w_in, conv_w, conv_b, ln_g, ln_b, w_out):
    row = lambda v: v.reshape(1, D_MODEL)
    n_col = D_MODEL // LANES
    conv_w = conv_w.reshape(CONF_WIDTH, n_col, LANES).transpose(1, 0, 2)
    conv_b = conv_b.reshape(n_col, 1, LANES)
    return pl.pallas_call(
        _mixb_body,
        name="conformer_mixer",
        grid=(T // TM,),
        in_specs=[_row_spec(D_MODEL), _const_spec((1, D_MODEL)),
                  _const_spec((D_MODEL, 2 * D_MODEL)), _const_spec((n_col, CONF_WIDTH, LANES)),
                  _const_spec((n_col, 1, LANES)), _const_spec((1, D_MODEL)), _const_spec((1, D_MODEL)),
                  _const_spec((D_MODEL, D_MODEL))],
        out_specs=_row_spec(D_MODEL),
        out_shape=jax.ShapeDtypeStruct((T, D_MODEL), F32),
        scratch_shapes=[pltpu.VMEM((n_col, TM + CONF_HALO, LANES), F32),
                        pltpu.VMEM((n_col, TM, LANES), F32)],
        compiler_params=_params(("arbitrary",)),
    )(h, row(g), w_in, conv_w, conv_b, row(ln_g), row(ln_b), w_out)


def _head_lanes(lane, e):
    chan = (lane >= e * HEAD_DIM) & (lane < (e + 1) * HEAD_DIM)
    a = LANES + AUG_STRIDE * e
    return chan | ((lane >= a) & (lane < a + 2 * N_SPLIT))


def _fox_proj_body(h_ref, g_ref, wq_ref, wk_ref, wv_ref, wf_ref, bf_ref, gq_ref, gk_ref,
                   tri_ref, selq_ref, selk_ref, qx_ref, kx_ref, vx_ref, carry_ref):
    t = pl.program_id(0) % NT

    @pl.when(t == 0)
    def _():
        carry_ref[...] = jnp.zeros_like(carry_ref)

    u = _rms(h_ref[...], g_ref[...]).astype(BF16)
    real = _row_in_batch(t) >= PAD

    f_logit = _dot(u, wf_ref[...]) + bf_ref[...]
    log_f = jnp.minimum(f_logit, 0.0) - jnp.log1p(jnp.exp(-jnp.abs(f_logit)))
    log_f = jnp.where(real, log_f, 0.0)
    c3 = _dot(tri_ref[...], jnp.concatenate(_split3(log_f), axis=1))
    cum = (c3[:, :LANES] + c3[:, LANES:2 * LANES] + c3[:, 2 * LANES:]) + carry_ref[0:1, :]
    carry_ref[...] = jnp.broadcast_to(cum[TM - 1:TM, :], carry_ref.shape)

    lane = lax.broadcasted_iota(jnp.int32, (1, LANES), 1)
    sub = lane % AUG_STRIDE
    in_aug = lane < 2 * AUG_STRIDE
    ones_q = (in_aug & (sub >= N_SPLIT) & (sub < 2 * N_SPLIT)).astype(F32)
    ones_k = (in_aug & (sub < N_SPLIT)).astype(F32)
    ones_v = (lane == 0).astype(BF16)
    ones_v = jnp.broadcast_to(ones_v, (TM, LANES))

    neg_ck = jnp.where(real, -LOG2E * cum, MASK_VALUE)
    aug_q = _dot(jnp.concatenate(_split3(LOG2E * cum), axis=1), selq_ref[...])
    aug_k = _dot(jnp.concatenate(_split3(neg_ck), axis=1), selk_ref[...])

    first_head = lane < HEAD_DIM
    gq = gq_ref[...] * (LOG2E * HEAD_DIM ** -0.5)
    gk = gk_ref[...]

    def normed(x, gain):
        x2 = x * x
        ms0 = jnp.sum(jnp.where(first_head, x2, 0.0), axis=-1, keepdims=True)
        ms1 = jnp.sum(jnp.where(first_head, 0.0, x2), axis=-1, keepdims=True)
        ms = jnp.where(first_head, ms0, ms1) * (1.0 / HEAD_DIM)
        return x * lax.rsqrt(ms + EPS) * gain

    q = _dot(u, wq_ref[...])
    k = _dot(u, wk_ref[...])
    v = _dot(u, wv_ref[...])
    for j in range(N_PAIRS):
        src = slice(j * LANES, (j + 1) * LANES)
        chan = slice(j * PAIR_W, j * PAIR_W + LANES)
        bias = slice(j * PAIR_W + LANES, (j + 1) * PAIR_W)
        qx_ref[:, chan] = normed(q[:, src], gq).astype(BF16)
        qx_ref[:, bias] = (aug_q[:, src] + ones_q).astype(BF16)
        kx_ref[:, chan] = normed(k[:, src], gk).astype(BF16)
        kx_ref[:, bias] = (aug_k[:, src] + ones_k).astype(BF16)
        vx_ref[:, chan] = v[:, src].astype(BF16)
        vx_ref[:, bias] = ones_v


def _fox_proj(h, g, wq, wk, wv, wf, bf, gq, gk, tri, selq, selk):
    out = jax.ShapeDtypeStruct((T, D_X), BF16)
    w_spec = _const_spec((D_MODEL, D_MODEL))
    sel_spec = _const_spec((N_SPLIT * LANES, D_MODEL))
    return pl.pallas_call(
        _fox_proj_body,
        name="fox_proj",
        grid=(T // TM,),
        in_specs=[_row_spec(D_MODEL), _const_spec((1, D_MODEL)), w_spec, w_spec, w_spec,
                  _const_spec((D_MODEL, LANES)), _const_spec((1, LANES)),
                  _const_spec((1, LANES)), _const_spec((1, LANES)),
                  _const_spec((TM, TM)), sel_spec, sel_spec],
        out_specs=[_row_spec(D_X)] * 3,
        out_shape=[out] * 3,
        scratch_shapes=[pltpu.VMEM((8, LANES), F32)],
        compiler_params=_params(("arbitrary",)),
    )(h, g.reshape(1, D_MODEL), wq, wk, wv, wf, bf, gq, gk, tri, selq, selk)


def _fox_attn_body(q_ref, k_ref, v_ref, o_ref, s_ref, m_ref, l_ref, acc_ref):
    i = pl.program_id(2)
    lane_x = lax.broadcasted_iota(jnp.int32, (1, PAIR_W), 1)
    first_head = lax.broadcasted_iota(jnp.int32, (1, LANES), 1) < HEAD_DIM
    heads = [(g, e) for g in range(ATTN_PAIRS) for e in range(2)]
    q_heads = []
    for g, e in heads:
        qx = q_ref[0, :, g * PAIR_W:(g + 1) * PAIR_W]
        q_heads.append(jnp.where(_head_lanes(lane_x, e), qx, jnp.zeros_like(qx)))

    def span(kb, n, ends_on_diagonal=False):
        keys = pl.ds(pl.multiple_of(kb * TM, TM), n * TM)
        for hd, (g, e) in enumerate(heads):
            k = k_ref[0, keys, g * PAIR_W:(g + 1) * PAIR_W]
            s = lax.dot_general(q_heads[hd], k, (((1,), (1,)), ((), ())), preferred_element_type=F32)
            if ends_on_diagonal:
                row = lax.broadcasted_iota(jnp.int32, (TM, n * TM), 0)
                col = lax.broadcasted_iota(jnp.int32, (TM, n * TM), 1)
                s = jnp.where(col <= row + (n - 1) * TM, s, MASK_VALUE)
            s_ref[hd, :, :n * TM] = s
        for g in range(ATTN_PAIRS):
            v = v_ref[0, keys, g * PAIR_W:(g + 1) * PAIR_W]
            alpha, pv = [], []
            for e in range(2):
                hd = 2 * g + e
                s = s_ref[hd, :, :n * TM]
                m_old = m_ref[hd]
                m_new = jnp.maximum(m_old, jnp.max(s, axis=-1, keepdims=True))
                p = jnp.exp2(s - m_new).astype(BF16)
                m_ref[hd] = m_new
                alpha.append(jnp.exp2(m_old - m_new))
                pv.append(_dot(p, v))
                l_ref[hd] = alpha[e] * l_ref[hd] + pv[e][:, LANES:]
            acc_ref[g] = (jnp.where(first_head, alpha[0], alpha[1]) * acc_ref[g]
                          + jnp.where(first_head, pv[0][:, :LANES], pv[1][:, :LANES]))

    m_ref[...] = jnp.full(m_ref.shape, MASK_VALUE, F32)
    l_ref[...] = jnp.zeros(l_ref.shape, F32)
    acc_ref[...] = jnp.zeros(acc_ref.shape, F32)

    @pl.when(i % 2 == 0)
    def _():
        span(i, 1, ends_on_diagonal=True)

    @pl.when(i % 2 == 1)
    def _():
        span(i - 1, 2, ends_on_diagonal=True)

    def step(u, carry):
        span(2 * u, 2)
        return carry

    lax.fori_loop(0, i // 2, step, 0)
    for g in range(ATTN_PAIRS):
        denom = jnp.where(first_head, l_ref[2 * g, :, 0:1], l_ref[2 * g + 1, :, 0:1])
        o_ref[0, :, g * LANES:(g + 1) * LANES] = (acc_ref[g] * (1.0 / denom)).astype(BF16)


def _fox_attn(qx, kx, vx):
    shape3 = (BATCH, LP, D_X)
    n_heads = 2 * ATTN_PAIRS
    q_spec = pl.BlockSpec((1, TM, ATTN_PAIRS * PAIR_W), lambda b, p, i: (b, i, p))
    kv_spec = pl.BlockSpec((1, LP, ATTN_PAIRS * PAIR_W), lambda b, p, i: (b, 0, p),
                           pipeline_mode=pl.Buffered(1))
    return pl.pallas_call(
        _fox_attn_body,
        name="fox_attn",
        grid=(BATCH, N_PAIRS // ATTN_PAIRS, NT),
        in_specs=[q_spec, kv_spec, kv_spec],
        out_specs=pl.BlockSpec((1, TM, ATTN_PAIRS * LANES), lambda b, p, i: (b, i, p)),
        out_shape=jax.ShapeDtypeStruct((BATCH, LP, D_MODEL), BF16),
        scratch_shapes=[pltpu.VMEM((n_heads, TM, 2 * TM), F32),
                        pltpu.VMEM((n_heads, TM, 1), F32),
                        pltpu.VMEM((n_heads, TM, LANES), F32),
                        pltpu.VMEM((ATTN_PAIRS, TM, LANES), F32)],
        compiler_params=_params(("arbitrary", "arbitrary", "arbitrary")),
    )(qx.reshape(shape3), kx.reshape(shape3), vx.reshape(shape3)).reshape(T, D_MODEL)


def _fox_out_body(h_ref, o_ref, w_ref, out_ref):
    out_ref[...] = h_ref[...] + _dot(o_ref[...], w_ref[...])


def _fox_out(h, o, w_out):
    return pl.pallas_call(
        _fox_out_body,
        name="fox_out",
        grid=(T // TM,),
        in_specs=[_row_spec(D_MODEL), _row_spec(D_MODEL), _const_spec((D_MODEL, D_MODEL))],
        out_specs=_row_spec(D_MODEL),
        out_shape=jax.ShapeDtypeStruct((T, D_MODEL), F32),
        compiler_params=_params(("arbitrary",)),
    )(h, o, w_out)


def _bias_selector(first_lane):
    r = jnp.arange(N_SPLIT * LANES)
    piece, head = r // LANES, r % LANES
    col = (head // 2) * LANES + (head % 2) * AUG_STRIDE + first_lane + piece
    hit = (jnp.arange(N_PAIRS * LANES)[None, :] == col[:, None]) & (head < N_HEADS)[:, None]
    return hit.astype(BF16)


def _fox(h, g, w_in, b_f, q_g, k_g, w_out):
    wq = w_in[:, :D_MODEL].astype(BF16)
    wk = w_in[:, D_MODEL:2 * D_MODEL].astype(BF16)
    wv = w_in[:, 2 * D_MODEL:3 * D_MODEL].astype(BF16)
    wf = jnp.pad(w_in[:, 3 * D_MODEL:], ((0, 0), (0, LANES - N_HEADS))).astype(BF16)
    bf = jnp.pad(b_f, (0, LANES - N_HEADS)).reshape(1, LANES)
    gq = jnp.tile(q_g, 2).reshape(1, LANES)
    gk = jnp.tile(k_g, 2).reshape(1, LANES)
    tri = (jnp.arange(TM)[None, :] <= jnp.arange(TM)[:, None]).astype(BF16)
    qx, kx, vx = _fox_proj(h, g, wq, wk, wv, wf, bf, gq, gk, tri,
                           _bias_selector(0), _bias_selector(N_SPLIT))
    return _fox_out(h, _fox_attn(qx, kx, vx), w_out.astype(BF16))


def kernel(x, meta, ffn_norm, ffn_w_gate, ffn_w_up, ffn_w_down, mix_norm, a_w_in, a_conv, a_w_out, b_w_in, b_conv, b_conv_bias, b_ln_g, b_ln_b, b_w_out, c_w_in, c_b_f, c_q_norm, c_k_norm, c_w_out):
    front = jnp.concatenate([jnp.zeros((PAD, D_MODEL), x.dtype), meta.astype(x.dtype)], axis=0)
    ffn_params = (ffn_norm.reshape(DEPTH, 2, 1, D_MODEL), ffn_w_gate.astype(BF16),
                  ffn_w_up.astype(BF16), ffn_w_down.astype(BF16))
    mixa_params = (mix_norm.reshape(DEPTH, 1, D_MODEL), a_w_in.astype(BF16), a_conv,
                   a_w_out.astype(BF16))
    h = None
    for i in range(DEPTH):
        m, j = i % N_MIXERS, i // N_MIXERS
        h = _ffn_first(x, front, (i, 0), *ffn_params) if i == 0 else _ffn(h, (i, 0), *ffn_params)
        if m == 0:
            h = _mixa(h, i, j, *mixa_params)
        elif m == 1:
            h = _mixb(h, mix_norm[i], b_w_in[j].astype(BF16), b_conv[j], b_conv_bias[j],
                      b_ln_g[j], b_ln_b[j], b_w_out[j].astype(BF16))
        else:
            h = _fox(h, mix_norm[i], c_w_in[j], c_b_f[j], c_q_norm[j], c_k_norm[j], c_w_out[j])
        h = _ffn_last(h, (i, 1), *ffn_params) if i == DEPTH - 1 else _ffn(h, (i, 1), *ffn_params)
    return h
```

```python
import jax
import jax.numpy as jnp
from jax import lax
from jax.experimental import pallas as pl
from jax.experimental.pallas import tpu as pltpu

D_MODEL = 1024
BATCH = 2
SEQ = 8192
DEPTH = 4
N_MIXERS = 3
N_META = 16
D_FF = 2816
SC_WIDTH = 3
CONF_WIDTH = 31
N_HEADS = 16
HEAD_DIM = D_MODEL // N_HEADS
EPS = 1e-6
MASK_VALUE = -1e30
LOG2E = 1.4426950408889634

LANES = 128
PAD = (-(N_META + SEQ)) % LANES
LP = PAD + N_META + SEQ
T = BATCH * LP
FRONT = PAD + N_META
TM = 640
NT = LP // TM
TM_LAST = 512
N_PAIRS = N_HEADS // 2
PAIR_W = 2 * LANES
D_X = N_PAIRS * PAIR_W
N_SPLIT = 3
AUG_STRIDE = 8
ATTN_PAIRS = 2
VT_ROWS = HEAD_DIM + 16
SC_HALO = 8
CONF_HALO = 32
CONF_ROWS = 64
VMEM_LIMIT = 56 * 1024 * 1024

F32 = jnp.float32
BF16 = jnp.bfloat16


def _rms(x, g):
    return x * lax.rsqrt(jnp.mean(x * x, axis=-1, keepdims=True) + EPS) * g


def _sigmoid(x):
    return 1.0 / (1.0 + jnp.exp(-x))


def _dot(a, b):
    return jnp.dot(a, b, preferred_element_type=F32)


def _split3(x):
    hi = x.astype(BF16)
    r1 = x - hi.astype(F32)
    mid = r1.astype(BF16)
    lo = (r1 - mid.astype(F32)).astype(BF16)
    return hi, mid, lo


def _row_in_batch(tile_in_batch):
    return tile_in_batch * TM + lax.broadcasted_iota(jnp.int32, (TM, 1), 0)


def _const_spec(shape):
    return pl.BlockSpec(shape, lambda *_: (0,) * len(shape), pipeline_mode=pl.Buffered(1))


def _picked_spec(pick, tail):
    return pl.BlockSpec((None,) * len(pick) + tail, lambda *_: tuple(pick) + (0,) * len(tail),
                        pipeline_mode=pl.Buffered(1))


def _row_spec(width):
    return pl.BlockSpec((TM, width), lambda i: (i, 0))


def _params(sem):
    return pltpu.CompilerParams(dimension_semantics=sem, vmem_limit_bytes=VMEM_LIMIT)


def _ffn_rows(h, g_ref, wg_ref, wu_ref, wd_ref):
    x = _rms(h, g_ref[...]).astype(BF16)
    gate = _dot(x, wg_ref[...])
    up = _dot(x, wu_ref[...])
    a = (gate * _sigmoid(gate) * up).astype(BF16)
    return h + 0.5 * _dot(a, wd_ref[...])


def _ffn_body(h_ref, g_ref, wg_ref, wu_ref, wd_ref, o_ref):
    o_ref[...] = _ffn_rows(h_ref[...], g_ref, wg_ref, wu_ref, wd_ref)


def _ffn_first_body(x_ref, front_ref, g_ref, wg_ref, wu_ref, wd_ref, o_ref):
    xb = x_ref[...]
    head = jnp.concatenate([front_ref[...], xb[:TM - FRONT]], axis=0)
    h = jnp.where(pl.program_id(1) == 0, head, xb)
    o_ref[...] = _ffn_rows(h, g_ref, wg_ref, wu_ref, wd_ref)


def _ffn_weight_specs(pick):
    return [_picked_spec(pick, (1, D_MODEL)), _picked_spec(pick, (D_MODEL, D_FF)),
            _picked_spec(pick, (D_MODEL, D_FF)), _picked_spec(pick, (D_FF, D_MODEL))]


def _ffn(h, pick, g, wg, wu, wd):
    return pl.pallas_call(
        _ffn_body,
        name="ffn",
        grid=(T // TM,),
        in_specs=[_row_spec(D_MODEL)] + _ffn_weight_specs(pick),
        out_specs=_row_spec(D_MODEL),
        out_shape=jax.ShapeDtypeStruct((T, D_MODEL), F32),
        compiler_params=_params(("arbitrary",)),
    )(h, g, wg, wu, wd)


def _ffn_first(x, front, pick, g, wg, wu, wd):
    x_spec = pl.BlockSpec(
        (pl.Element(TM), pl.Element(D_MODEL)),
        lambda b, t: (pl.multiple_of(b * SEQ + jnp.maximum(t * TM - FRONT, 0), LANES), 0))
    return pl.pallas_call(
        _ffn_first_body,
        name="ffn_first",
        grid=(BATCH, NT),
        in_specs=[x_spec, _const_spec((FRONT, D_MODEL))] + _ffn_weight_specs(pick),
        out_specs=pl.BlockSpec((TM, D_MODEL), lambda b, t: (b * NT + t, 0)),
        out_shape=jax.ShapeDtypeStruct((T, D_MODEL), F32),
        compiler_params=_params(("arbitrary", "arbitrary")),
    )(x.reshape(BATCH * SEQ, D_MODEL), front, g, wg, wu, wd)


def _ffn_last(h, pick, g, wg, wu, wd):
    n_tiles = SEQ // TM_LAST
    h_spec = pl.BlockSpec(
        (pl.Element(TM_LAST), pl.Element(D_MODEL)),
        lambda b, t: (pl.multiple_of(b * LP + FRONT + t * TM_LAST, LANES), 0))
    return pl.pallas_call(
        _ffn_body,
        name="ffn_last",
        grid=(BATCH, n_tiles),
        in_specs=[h_spec] + _ffn_weight_specs(pick),
        out_specs=pl.BlockSpec((TM_LAST, D_MODEL), lambda b, t: (b * n_tiles + t, 0)),
        out_shape=jax.ShapeDtypeStruct((BATCH * SEQ, D_MODEL), F32),
        compiler_params=_params(("arbitrary", "arbitrary")),
    )(h, g, wg, wu, wd).reshape(BATCH, SEQ, D_MODEL)


def _mixa_body(h_ref, g_ref, win_ref, cw_ref, wout_ref, o_ref, buf_ref):
    t = pl.program_id(0) % NT

    @pl.when(t == 0)
    def _():
        buf_ref[0:SC_HALO, :] = jnp.zeros((SC_HALO, D_MODEL), F32)

    h = h_ref[...]
    u = _rms(h, g_ref[...]).astype(BF16)
    p = _dot(u, win_ref[...])
    b_gate = p[:, :D_MODEL]
    cv = p[:, D_MODEL:2 * D_MODEL] * p[:, 2 * D_MODEL:]
    cv = jnp.where(_row_in_batch(t) >= PAD, cv, 0.0)
    buf_ref[SC_HALO:SC_HALO + TM, :] = cv
    y = cw_ref[SC_WIDTH - 1:SC_WIDTH, :] * cv
    for k in range(SC_WIDTH - 1):
        shift = SC_WIDTH - 1 - k
        y = y + cw_ref[k:k + 1, :] * buf_ref[SC_HALO - shift:SC_HALO - shift + TM, :]
    z = (b_gate * y).astype(BF16)
    o_ref[...] = h + _dot(z, wout_ref[...])
    buf_ref[0:SC_HALO, :] = buf_ref[TM:TM + SC_HALO, :]


def _mixa(h, layer, j, g, w_in, conv_w, w_out):
    return pl.pallas_call(
        _mixa_body,
        name="short_conv_mixer",
        grid=(T // TM,),
        in_specs=[_row_spec(D_MODEL), _picked_spec((layer,), (1, D_MODEL)),
                  _picked_spec((j,), (D_MODEL, 3 * D_MODEL)), _picked_spec((j,), (SC_WIDTH, D_MODEL)),
                  _picked_spec((j,), (D_MODEL, D_MODEL))],
        out_specs=_row_spec(D_MODEL),
        out_shape=jax.ShapeDtypeStruct((T, D_MODEL), F32),
        scratch_shapes=[pltpu.VMEM((TM + SC_HALO, D_MODEL), F32)],
        compiler_params=_params(("arbitrary",)),
    )(h, g, w_in, conv_w, w_out)


def _mixb_body(h_ref, g_ref, win_ref, cw_ref, cb_ref, lng_ref, lnb_ref, wout_ref,
               o_ref, buf_ref, y_ref):
    t = pl.program_id(0) % NT

    @pl.when(t == 0)
    def _():
        buf_ref[:, 0:CONF_HALO, :] = jnp.zeros((D_MODEL // LANES, CONF_HALO, LANES), F32)

    h = h_ref[...]
    u = _rms(h, g_ref[...]).astype(BF16)
    p = _dot(u, win_ref[...])
    glu = p[:, :D_MODEL] * _sigmoid(p[:, D_MODEL:])
    glu = jnp.where(_row_in_batch(t) >= PAD, glu, 0.0)
    n_col = D_MODEL // LANES
    for c in range(n_col):
        buf_ref[c, CONF_HALO:CONF_HALO + TM, :] = glu[:, c * LANES:(c + 1) * LANES]

    first = CONF_HALO - (CONF_WIDTH - 1)

    def lane_tile(c, carry):
        w = cw_ref[c]
        bias = jnp.broadcast_to(cb_ref[c], (CONF_ROWS, LANES))
        for r in range(0, TM, CONF_ROWS):
            acc = bias
            for k in range(CONF_WIDTH):
                acc = acc + w[k:k + 1, :] * buf_ref[c, r + first + k:r + first + k + CONF_ROWS, :]
            y_ref[c, r:r + CONF_ROWS, :] = acc
        buf_ref[c, 0:CONF_HALO, :] = buf_ref[c, TM:TM + CONF_HALO, :]
        return carry

    lax.fori_loop(0, n_col, lane_tile, 0)

    y = jnp.concatenate([y_ref[c] for c in range(n_col)], axis=1)
    mu = jnp.mean(y, axis=-1, keepdims=True)
    yc = y - mu
    var = jnp.mean(yc * yc, axis=-1, keepdims=True)
    ln = yc * lax.rsqrt(var + EPS) * lng_ref[...] + lnb_ref[...]
    s = (ln * _sigmoid(ln)).astype(BF16)
    o_ref[...] = h + _dot(s, wout_ref[...])


def _mixb(h, g, w_in, conv_w, conv_b, ln_g, ln_b, w_out):
    row = lambda v: v.reshape(1, D_MODEL)
    n_col = D_MODEL // LANES
    conv_w = conv_w.reshape(CONF_WIDTH, n_col, LANES).transpose(1, 0, 2)
    conv_b = conv_b.reshape(n_col, 1, LANES)
    return pl.pallas_call(
        _mixb_body,
        name="conformer_mixer",
        grid=(T // TM,),
        in_specs=[_row_spec(D_MODEL), _const_spec((1, D_MODEL)),
                  _const_spec((D_MODEL, 2 * D_MODEL)), _const_spec((n_col, CONF_WIDTH, LANES)),
                  _const_spec((n_col, 1, LANES)), _const_spec((1, D_MODEL)), _const_spec((1, D_MODEL)),
                  _const_spec((D_MODEL, D_MODEL))],
        out_specs=_row_spec(D_MODEL),
        out_shape=jax.ShapeDtypeStruct((T, D_MODEL), F32),
        scratch_shapes=[pltpu.VMEM((n_col, TM + CONF_HALO, LANES), F32),
                        pltpu.VMEM((n_col, TM, LANES), F32)],
        compiler_params=_params(("arbitrary",)),
    )(h, row(g), w_in, conv_w, conv_b, row(ln_g), row(ln_b), w_out)


def _head_lanes(lane, e):
    chan = (lane >= e * HEAD_DIM) & (lane < (e + 1) * HEAD_DIM)
    a = LANES + AUG_STRIDE * e
    return chan | ((lane >= a) & (lane < a + 2 * N_SPLIT))


def _fox_proj_body(h_ref, g_ref, wq_ref, wk_ref, wv_ref, wf_ref, bf_ref, gq_ref, gk_ref,
                   tri_ref, selq_ref, selk_ref, qx_ref, kx_ref, vt_ref, carry_ref):
    t = pl.program_id(0) % NT

    @pl.when(t == 0)
    def _():
        carry_ref[...] = jnp.zeros_like(carry_ref)

    u = _rms(h_ref[...], g_ref[...]).astype(BF16)
    real = _row_in_batch(t) >= PAD

    f_logit = _dot(u, wf_ref[...]) + bf_ref[...]
    log_f = jnp.minimum(f_logit, 0.0) - jnp.log1p(jnp.exp(-jnp.abs(f_logit)))
    log_f = jnp.where(real, log_f, 0.0)
    c3 = _dot(tri_ref[...], jnp.concatenate(_split3(log_f), axis=1))
    cum = (c3[:, :LANES] + c3[:, LANES:2 * LANES] + c3[:, 2 * LANES:]) + carry_ref[0:1, :]
    carry_ref[...] = jnp.broadcast_to(cum[TM - 1:TM, :], carry_ref.shape)

    lane = lax.broadcasted_iota(jnp.int32, (1, LANES), 1)
    sub = lane % AUG_STRIDE
    in_aug = lane < 2 * AUG_STRIDE
    ones_q = (in_aug & (sub >= N_SPLIT) & (sub < 2 * N_SPLIT)).astype(F32)
    ones_k = (in_aug & (sub < N_SPLIT)).astype(F32)
    pad_rows = VT_ROWS - HEAD_DIM
    ones_row = (lax.broadcasted_iota(jnp.int32, (pad_rows, TM), 0) == 0).astype(BF16)

    neg_ck = jnp.where(real, -LOG2E * cum, MASK_VALUE)
    aug_q = _dot(jnp.concatenate(_split3(LOG2E * cum), axis=1), selq_ref[...])
    aug_k = _dot(jnp.concatenate(_split3(neg_ck), axis=1), selk_ref[...])

    first_head = lane < HEAD_DIM
    gq = gq_ref[...] * (LOG2E * HEAD_DIM ** -0.5)
    gk = gk_ref[...]

    def normed(x, gain):
        x2 = x * x
        ms0 = jnp.sum(jnp.where(first_head, x2, 0.0), axis=-1, keepdims=True)
        ms1 = jnp.sum(jnp.where(first_head, 0.0, x2), axis=-1, keepdims=True)
        ms = jnp.where(first_head, ms0, ms1) * (1.0 / HEAD_DIM)
        return x * lax.rsqrt(ms + EPS) * gain

    q = _dot(u, wq_ref[...])
    k = _dot(u, wk_ref[...])
    v = _dot(u, wv_ref[...])
    for j in range(N_PAIRS):
        src = slice(j * LANES, (j + 1) * LANES)
        chan = slice(j * PAIR_W, j * PAIR_W + LANES)
        bias = slice(j * PAIR_W + LANES, (j + 1) * PAIR_W)
        qx_ref[:, chan] = normed(q[:, src], gq).astype(BF16)
        qx_ref[:, bias] = (aug_q[:, src] + ones_q).astype(BF16)
        kx_ref[:, chan] = normed(k[:, src], gk).astype(BF16)
        kx_ref[:, bias] = (aug_k[:, src] + ones_k).astype(BF16)
        v_t = v[:, src].T
        for e in range(2):
            vt_ref[2 * j + e, :HEAD_DIM, :] = v_t[e * HEAD_DIM:(e + 1) * HEAD_DIM].astype(BF16)
            vt_ref[2 * j + e, HEAD_DIM:, :] = ones_row


def _fox_proj(h, g, wq, wk, wv, wf, bf, gq, gk, tri, selq, selk):
    out = jax.ShapeDtypeStruct((T, D_X), BF16)
    w_spec = _const_spec((D_MODEL, D_MODEL))
    sel_spec = _const_spec((N_SPLIT * LANES, D_MODEL))
    return pl.pallas_call(
        _fox_proj_body,
        name="fox_proj",
        grid=(T // TM,),
        in_specs=[_row_spec(D_MODEL), _const_spec((1, D_MODEL)), w_spec, w_spec, w_spec,
                  _const_spec((D_MODEL, LANES)), _const_spec((1, LANES)),
                  _const_spec((1, LANES)), _const_spec((1, LANES)),
                  _const_spec((TM, TM)), sel_spec, sel_spec],
        out_specs=[_row_spec(D_X), _row_spec(D_X),
                   pl.BlockSpec((None, N_HEADS, None, VT_ROWS, TM), lambda i: (i // NT, 0, i % NT, 0, 0))],
        out_shape=[out, out, jax.ShapeDtypeStruct((BATCH, N_HEADS, NT, VT_ROWS, TM), BF16)],
        scratch_shapes=[pltpu.VMEM((8, LANES), F32)],
        compiler_params=_params(("arbitrary",)),
    )(h, g.reshape(1, D_MODEL), wq, wk, wv, wf, bf, gq, gk, tri, selq, selk)


def _fox_attn_body(q_ref, k_ref, vt_ref, o_ref, s_ref, m_ref, acc_ref):
    i = pl.program_id(2)
    lane_x = lax.broadcasted_iota(jnp.int32, (1, PAIR_W), 1)
    heads = [(g, e) for g in range(ATTN_PAIRS) for e in range(2)]
    q_heads = []
    for g, e in heads:
        qx = q_ref[0, :, g * PAIR_W:(g + 1) * PAIR_W]
        q_heads.append(jnp.where(_head_lanes(lane_x, e), qx, jnp.zeros_like(qx)))

    def span(kb, n, ends_on_diagonal=False):
        keys = pl.ds(pl.multiple_of(kb * TM, TM), n * TM)
        for hd, (g, e) in enumerate(heads):
            k = k_ref[0, keys, g * PAIR_W:(g + 1) * PAIR_W]
            s = lax.dot_general(k, q_heads[hd], (((1,), (1,)), ((), ())), preferred_element_type=F32)
            if ends_on_diagonal:
                key = lax.broadcasted_iota(jnp.int32, (n * TM, TM), 0)
                qry = lax.broadcasted_iota(jnp.int32, (n * TM, TM), 1)
                s = jnp.where(key <= qry + (n - 1) * TM, s, MASK_VALUE)
            s_ref[hd, :n * TM, :] = s
        for hd in range(len(heads)):
            s = s_ref[hd, :n * TM, :]
            m_old = m_ref[hd]
            m_new = jnp.maximum(m_old, jnp.max(s, axis=0, keepdims=True))
            p = jnp.exp2(s - m_new).astype(BF16)
            m_ref[hd] = m_new
            v_t = jnp.concatenate([vt_ref[hd, kb + b] for b in range(n)], axis=1)
            acc_ref[hd] = jnp.exp2(m_old - m_new) * acc_ref[hd] + _dot(v_t, p)

    m_ref[...] = jnp.full(m_ref.shape, MASK_VALUE, F32)
    acc_ref[...] = jnp.zeros(acc_ref.shape, F32)

    @pl.when(i % 2 == 0)
    def _():
        span(i, 1, ends_on_diagonal=True)

    @pl.when(i % 2 == 1)
    def _():
        span(i - 1, 2, ends_on_diagonal=True)

    def step(u, carry):
        span(2 * u, 2)
        return carry

    lax.fori_loop(0, i // 2, step, 0)
    for g in range(ATTN_PAIRS):
        pair = []
        for e in range(2):
            acc = acc_ref[2 * g + e]
            pair.append(acc[:HEAD_DIM] * (1.0 / acc[HEAD_DIM:HEAD_DIM + 1]))
        o_ref[0, :, g * LANES:(g + 1) * LANES] = jnp.concatenate(pair, axis=0).T.astype(BF16)


def _fox_attn(qx, kx, vt):
    shape3 = (BATCH, LP, D_X)
    n_heads = 2 * ATTN_PAIRS
    q_spec = pl.BlockSpec((1, TM, ATTN_PAIRS * PAIR_W), lambda b, p, i: (b, i, p))
    k_spec = pl.BlockSpec((1, LP, ATTN_PAIRS * PAIR_W), lambda b, p, i: (b, 0, p),
                          pipeline_mode=pl.Buffered(1))
    vt_spec = pl.BlockSpec((None, n_heads, NT, VT_ROWS, TM), lambda b, p, i: (b, p, 0, 0, 0),
                           pipeline_mode=pl.Buffered(1))
    return pl.pallas_call(
        _fox_attn_body,
        name="fox_attn",
        grid=(BATCH, N_PAIRS // ATTN_PAIRS, NT),
        in_specs=[q_spec, k_spec, vt_spec],
        out_specs=pl.BlockSpec((1, TM, ATTN_PAIRS * LANES), lambda b, p, i: (b, i, p)),
        out_shape=jax.ShapeDtypeStruct((BATCH, LP, D_MODEL), BF16),
        scratch_shapes=[pltpu.VMEM((n_heads, 2 * TM, TM), F32),
                        pltpu.VMEM((n_heads, 1, TM), F32),
                        pltpu.VMEM((n_heads, VT_ROWS, TM), F32)],
        compiler_params=_params(("arbitrary", "arbitrary", "arbitrary")),
    )(qx.reshape(shape3), kx.reshape(shape3), vt).reshape(T, D_MODEL)


def _fox_out_body(h_ref, o_ref, w_ref, out_ref):
    out_ref[...] = h_ref[...] + _dot(o_ref[...], w_ref[...])


def _fox_out(h, o, w_out):
    return pl.pallas_call(
        _fox_out_body,
        name="fox_out",
        grid=(T // TM,),
        in_specs=[_row_spec(D_MODEL), _row_spec(D_MODEL), _const_spec((D_MODEL, D_MODEL))],
        out_specs=_row_spec(D_MODEL),
        out_shape=jax.ShapeDtypeStruct((T, D_MODEL), F32),
        compiler_params=_params(("arbitrary",)),
    )(h, o, w_out)


def _bias_selector(first_lane):
    r = jnp.arange(N_SPLIT * LANES)
    piece, head = r // LANES, r % LANES
    col = (head // 2) * LANES + (head % 2) * AUG_STRIDE + first_lane + piece
    hit = (jnp.arange(N_PAIRS * LANES)[None, :] == col[:, None]) & (head < N_HEADS)[:, None]
    return hit.astype(BF16)


def _fox(h, g, w_in, b_f, q_g, k_g, w_out):
    wq = w_in[:, :D_MODEL].astype(BF16)
    wk = w_in[:, D_MODEL:2 * D_MODEL].astype(BF16)
    wv = w_in[:, 2 * D_MODEL:3 * D_MODEL].astype(BF16)
    wf = jnp.pad(w_in[:, 3 * D_MODEL:], ((0, 0), (0, LANES - N_HEADS))).astype(BF16)
    bf = jnp.pad(b_f, (0, LANES - N_HEADS)).reshape(1, LANES)
    gq = jnp.tile(q_g, 2).reshape(1, LANES)
    gk = jnp.tile(k_g, 2).reshape(1, LANES)
    tri = (jnp.arange(TM)[None, :] <= jnp.arange(TM)[:, None]).astype(BF16)
    qx, kx, vt = _fox_proj(h, g, wq, wk, wv, wf, bf, gq, gk, tri,
                           _bias_selector(0), _bias_selector(N_SPLIT))
    return _fox_out(h, _fox_attn(qx, kx, vt), w_out.astype(BF16))


def kernel(x, meta, ffn_norm, ffn_w_gate, ffn_w_up, ffn_w_down, mix_norm, a_w_in, a_conv, a_w_out, b_w_in, b_conv, b_conv_bias, b_ln_g, b_ln_b, b_w_out, c_w_in, c_b_f, c_q_norm, c_k_norm, c_w_out):
    front = jnp.concatenate([jnp.zeros((PAD, D_MODEL), x.dtype), meta.astype(x.dtype)], axis=0)
    ffn_params = (ffn_norm.reshape(DEPTH, 2, 1, D_MODEL), ffn_w_gate.astype(BF16),
                  ffn_w_up.astype(BF16), ffn_w_down.astype(BF16))
    mixa_params = (mix_norm.reshape(DEPTH, 1, D_MODEL), a_w_in.astype(BF16), a_conv,
                   a_w_out.astype(BF16))
    h = None
    for i in range(DEPTH):
        m, j = i % N_MIXERS, i // N_MIXERS
        h = _ffn_first(x, front, (i, 0), *ffn_params) if i == 0 else _ffn(h, (i, 0), *ffn_params)
        if m == 0:
            h = _mixa(h, i, j, *mixa_params)
        elif m == 1:
            h = _mixb(h, mix_norm[i], b_w_in[j].astype(BF16), b_conv[j], b_conv_bias[j],
                      b_ln_g[j], b_ln_b[j], b_w_out[j].astype(BF16))
        else:
            h = _fox(h, mix_norm[i], c_w_in[j], c_b_f[j], c_q_norm[j], c_k_norm[j], c_w_out[j])
        h = _ffn_last(h, (i, 1), *ffn_params) if i == DEPTH - 1 else _ffn(h, (i, 1), *ffn_params)
    return h
```

```python
import jax
import jax.numpy as jnp
from jax import lax
from jax.experimental import pallas as pl
from jax.experimental.pallas import tpu as pltpu

D_MODEL = 1024
BATCH = 2
SEQ = 8192
DEPTH = 4
N_MIXERS = 3
N_META = 16
D_FF = 2816
SC_WIDTH = 3
CONF_WIDTH = 31
N_HEADS = 16
HEAD_DIM = D_MODEL // N_HEADS
EPS = 1e-6
MASK_VALUE = -1e30
LOG2E = 1.4426950408889634

LANES = 128
PAD = (-(N_META + SEQ)) % LANES
LP = PAD + N_META + SEQ
T = BATCH * LP
FRONT = PAD + N_META
TM = 640
NT = LP // TM
TM_LAST = 512
N_PAIRS = N_HEADS // 2
PAIR_W = 2 * LANES
D_X = N_PAIRS * PAIR_W
N_SPLIT = 3
AUG_STRIDE = 8
ATTN_PAIRS = 2
VT_ROWS = HEAD_DIM + 16
SC_HALO = 8
CONF_HALO = 32
CONF_ROWS = 64
VMEM_LIMIT = 56 * 1024 * 1024

F32 = jnp.float32
BF16 = jnp.bfloat16


def _rms(x, g):
    return x * lax.rsqrt(jnp.mean(x * x, axis=-1, keepdims=True) + EPS) * g


def _sigmoid(x):
    return 1.0 / (1.0 + jnp.exp(-x))


def _dot(a, b):
    return jnp.dot(a, b, preferred_element_type=F32)


def _split3(x):
    hi = x.astype(BF16)
    r1 = x - hi.astype(F32)
    mid = r1.astype(BF16)
    lo = (r1 - mid.astype(F32)).astype(BF16)
    return hi, mid, lo


def _row_in_batch(tile_in_batch):
    return tile_in_batch * TM + lax.broadcasted_iota(jnp.int32, (TM, 1), 0)


def _const_spec(shape):
    return pl.BlockSpec(shape, lambda *_: (0,) * len(shape), pipeline_mode=pl.Buffered(1))


def _picked_spec(pick, tail):
    return pl.BlockSpec((None,) * len(pick) + tail, lambda *_: tuple(pick) + (0,) * len(tail),
                        pipeline_mode=pl.Buffered(1))


def _row_spec(width):
    return pl.BlockSpec((TM, width), lambda i: (i, 0))


def _params(sem):
    return pltpu.CompilerParams(dimension_semantics=sem, vmem_limit_bytes=VMEM_LIMIT)


def _ffn_rows(h, g_ref, wg_ref, wu_ref, wd_ref):
    x = _rms(h, g_ref[...]).astype(BF16)
    gate = _dot(x, wg_ref[...])
    up = _dot(x, wu_ref[...])
    a = (gate * _sigmoid(gate) * up).astype(BF16)
    return h + 0.5 * _dot(a, wd_ref[...])


def _ffn_body(h_ref, g_ref, wg_ref, wu_ref, wd_ref, o_ref):
    o_ref[...] = _ffn_rows(h_ref[...], g_ref, wg_ref, wu_ref, wd_ref)


def _ffn_first_body(x_ref, front_ref, g_ref, wg_ref, wu_ref, wd_ref, o_ref):
    xb = x_ref[...]
    head = jnp.concatenate([front_ref[...], xb[:TM - FRONT]], axis=0)
    h = jnp.where(pl.program_id(1) == 0, head, xb)
    o_ref[...] = _ffn_rows(h, g_ref, wg_ref, wu_ref, wd_ref)


def _ffn_weight_specs(pick):
    return [_picked_spec(pick, (1, D_MODEL)), _picked_spec(pick, (D_MODEL, D_FF)),
            _picked_spec(pick, (D_MODEL, D_FF)), _picked_spec(pick, (D_FF, D_MODEL))]


def _ffn(h, pick, g, wg, wu, wd):
    return pl.pallas_call(
        _ffn_body,
        name="ffn",
        grid=(T // TM,),
        in_specs=[_row_spec(D_MODEL)] + _ffn_weight_specs(pick),
        out_specs=_row_spec(D_MODEL),
        out_shape=jax.ShapeDtypeStruct((T, D_MODEL), F32),
        compiler_params=_params(("arbitrary",)),
    )(h, g, wg, wu, wd)


def _ffn_first(x, front, pick, g, wg, wu, wd):
    x_spec = pl.BlockSpec(
        (pl.Element(TM), pl.Element(D_MODEL)),
        lambda b, t: (pl.multiple_of(b * SEQ + jnp.maximum(t * TM - FRONT, 0), LANES), 0))
    return pl.pallas_call(
        _ffn_first_body,
        name="ffn_first",
        grid=(BATCH, NT),
        in_specs=[x_spec, _const_spec((FRONT, D_MODEL))] + _ffn_weight_specs(pick),
        out_specs=pl.BlockSpec((TM, D_MODEL), lambda b, t: (b * NT + t, 0)),
        out_shape=jax.ShapeDtypeStruct((T, D_MODEL), F32),
        compiler_params=_params(("arbitrary", "arbitrary")),
    )(x.reshape(BATCH * SEQ, D_MODEL), front, g, wg, wu, wd)


def _ffn_last(h, pick, g, wg, wu, wd):
    n_tiles = SEQ // TM_LAST
    h_spec = pl.BlockSpec(
        (pl.Element(TM_LAST), pl.Element(D_MODEL)),
        lambda b, t: (pl.multiple_of(b * LP + FRONT + t * TM_LAST, LANES), 0))
    return pl.pallas_call(
        _ffn_body,
        name="ffn_last",
        grid=(BATCH, n_tiles),
        in_specs=[h_spec] + _ffn_weight_specs(pick),
        out_specs=pl.BlockSpec((TM_LAST, D_MODEL), lambda b, t: (b * n_tiles + t, 0)),
        out_shape=jax.ShapeDtypeStruct((BATCH * SEQ, D_MODEL), F32),
        compiler_params=_params(("arbitrary", "arbitrary")),
    )(h, g, wg, wu, wd).reshape(BATCH, SEQ, D_MODEL)


def _mixa_body(h_ref, g_ref, win_ref, cw_ref, wout_ref, o_ref, buf_ref):
    t = pl.program_id(0) % NT

    @pl.when(t == 0)
    def _():
        buf_ref[0:SC_HALO, :] = jnp.zeros((SC_HALO, D_MODEL), F32)

    h = h_ref[...]
    u = _rms(h, g_ref[...]).astype(BF16)
    p = _dot(u, win_ref[...])
    b_gate = p[:, :D_MODEL]
    cv = p[:, D_MODEL:2 * D_MODEL] * p[:, 2 * D_MODEL:]
    cv = jnp.where(_row_in_batch(t) >= PAD, cv, 0.0)
    buf_ref[SC_HALO:SC_HALO + TM, :] = cv
    y = cw_ref[SC_WIDTH - 1:SC_WIDTH, :] * cv
    for k in range(SC_WIDTH - 1):
        shift = SC_WIDTH - 1 - k
        y = y + cw_ref[k:k + 1, :] * buf_ref[SC_HALO - shift:SC_HALO - shift + TM, :]
    z = (b_gate * y).astype(BF16)
    o_ref[...] = h + _dot(z, wout_ref[...])
    buf_ref[0:SC_HALO, :] = buf_ref[TM:TM + SC_HALO, :]


def _mixa(h, layer, j, g, w_in, conv_w, w_out):
    return pl.pallas_call(
        _mixa_body,
        name="short_conv_mixer",
        grid=(T // TM,),
        in_specs=[_row_spec(D_MODEL), _picked_spec((layer,), (1, D_MODEL)),
                  _picked_spec((j,), (D_MODEL, 3 * D_MODEL)), _picked_spec((j,), (SC_WIDTH, D_MODEL)),
                  _picked_spec((j,), (D_MODEL, D_MODEL))],
        out_specs=_row_spec(D_MODEL),
        out_shape=jax.ShapeDtypeStruct((T, D_MODEL), F32),
        scratch_shapes=[pltpu.VMEM((TM + SC_HALO, D_MODEL), F32)],
        compiler_params=_params(("arbitrary",)),
    )(h, g, w_in, conv_w, w_out)


def _mixb_body(h_ref, g_ref, win_ref, cw_ref, cb_ref, lng_ref, lnb_ref, wout_ref,
               o_ref, buf_ref, y_ref):
    t = pl.program_id(0) % NT

    @pl.when(t == 0)
    def _():
        buf_ref[:, 0:CONF_HALO, :] = jnp.zeros((D_MODEL // LANES, CONF_HALO, LANES), F32)

    h = h_ref[...]
    u = _rms(h, g_ref[...]).astype(BF16)
    real = _row_in_batch(t) >= PAD
    n_col = D_MODEL // LANES

    first = CONF_HALO - (CONF_WIDTH - 1)
    for c in range(n_col):
        p = _dot(u, win_ref[:, 2 * c * LANES:2 * (c + 1) * LANES])
        glu = jnp.where(real, p[:, :LANES] * _sigmoid(p[:, LANES:]), 0.0)
        buf_ref[c, CONF_HALO:CONF_HALO + TM, :] = glu
        w = cw_ref[c]
        bias = jnp.broadcast_to(cb_ref[c], (CONF_ROWS, LANES))
        for r in range(0, TM, CONF_ROWS):
            acc = bias
            for k in range(CONF_WIDTH):
                acc = acc + w[k:k + 1, :] * buf_ref[c, r + first + k:r + first + k + CONF_ROWS, :]
            y_ref[c, r:r + CONF_ROWS, :] = acc
        buf_ref[c, 0:CONF_HALO, :] = buf_ref[c, TM:TM + CONF_HALO, :]

    y = jnp.concatenate([y_ref[c] for c in range(n_col)], axis=1)
    mu = jnp.mean(y, axis=-1, keepdims=True)
    yc = y - mu
    var = jnp.mean(yc * yc, axis=-1, keepdims=True)
    ln = yc * lax.rsqrt(var + EPS) * lng_ref[...] + lnb_ref[...]
    s = (ln * _sigmoid(ln)).astype(BF16)
    o_ref[...] = h + _dot(s, wout_ref[...])


def _mixb(h, g, w_in, conv_w, conv_b, ln_g, ln_b, w_out):
    row = lambda v: v.reshape(1, D_MODEL)
    n_col = D_MODEL // LANES
    conv_w = conv_w.reshape(CONF_WIDTH, n_col, LANES).transpose(1, 0, 2)
    conv_b = conv_b.reshape(n_col, 1, LANES)
    w_in = w_in.reshape(D_MODEL, 2, n_col, LANES).transpose(0, 2, 1, 3).reshape(D_MODEL, 2 * D_MODEL)
    return pl.pallas_call(
        _mixb_body,
        name="conformer_mixer",
        grid=(T // TM,),
        in_specs=[_row_spec(D_MODEL), _const_spec((1, D_MODEL)),
                  _const_spec((D_MODEL, 2 * D_MODEL)), _const_spec((n_col, CONF_WIDTH, LANES)),
                  _const_spec((n_col, 1, LANES)), _const_spec((1, D_MODEL)), _const_spec((1, D_MODEL)),
                  _const_spec((D_MODEL, D_MODEL))],
        out_specs=_row_spec(D_MODEL),
        out_shape=jax.ShapeDtypeStruct((T, D_MODEL), F32),
        scratch_shapes=[pltpu.VMEM((n_col, TM + CONF_HALO, LANES), F32),
                        pltpu.VMEM((n_col, TM, LANES), F32)],
        compiler_params=_params(("arbitrary",)),
    )(h, row(g), w_in, conv_w, conv_b, row(ln_g), row(ln_b), w_out)


def _head_lanes(lane, e):
    chan = (lane >= e * HEAD_DIM) & (lane < (e + 1) * HEAD_DIM)
    a = LANES + AUG_STRIDE * e
    return chan | ((lane >= a) & (lane < a + 2 * N_SPLIT))


def _fox_proj_body(h_ref, g_ref, wq_ref, wk_ref, wv_ref, wf_ref, bf_ref, gq_ref, gk_ref,
                   tri_ref, selq_ref, selk_ref, qx_ref, kx_ref, vt_ref, carry_ref):
    t = pl.program_id(0) % NT

    @pl.when(t == 0)
    def _():
        carry_ref[...] = jnp.zeros_like(carry_ref)

    u = _rms(h_ref[...], g_ref[...]).astype(BF16)
    real = _row_in_batch(t) >= PAD

    f_logit = _dot(u, wf_ref[...]) + bf_ref[...]
    log_f = jnp.minimum(f_logit, 0.0) - jnp.log1p(jnp.exp(-jnp.abs(f_logit)))
    log_f = jnp.where(real, log_f, 0.0)
    c3 = _dot(tri_ref[...], jnp.concatenate(_split3(log_f), axis=1))
    cum = (c3[:, :LANES] + c3[:, LANES:2 * LANES] + c3[:, 2 * LANES:]) + carry_ref[0:1, :]
    carry_ref[...] = jnp.broadcast_to(cum[TM - 1:TM, :], carry_ref.shape)

    lane = lax.broadcasted_iota(jnp.int32, (1, LANES), 1)
    sub = lane % AUG_STRIDE
    in_aug = lane < 2 * AUG_STRIDE
    ones_q = (in_aug & (sub >= N_SPLIT) & (sub < 2 * N_SPLIT)).astype(F32)
    ones_k = (in_aug & (sub < N_SPLIT)).astype(F32)
    pad_rows = VT_ROWS - HEAD_DIM
    ones_row = (lax.broadcasted_iota(jnp.int32, (pad_rows, TM), 0) == 0).astype(BF16)

    neg_ck = jnp.where(real, -LOG2E * cum, MASK_VALUE)
    aug_q = _dot(jnp.concatenate(_split3(LOG2E * cum), axis=1), selq_ref[...])
    aug_k = _dot(jnp.concatenate(_split3(neg_ck), axis=1), selk_ref[...])

    first_head = lane < HEAD_DIM
    gq = gq_ref[...] * (LOG2E * HEAD_DIM ** -0.5)
    gk = gk_ref[...]

    def normed(x, gain):
        x2 = x * x
        ms0 = jnp.sum(jnp.where(first_head, x2, 0.0), axis=-1, keepdims=True)
        ms1 = jnp.sum(jnp.where(first_head, 0.0, x2), axis=-1, keepdims=True)
        ms = jnp.where(first_head, ms0, ms1) * (1.0 / HEAD_DIM)
        return x * lax.rsqrt(ms + EPS) * gain

    q = _dot(u, wq_ref[...])
    k = _dot(u, wk_ref[...])
    v = _dot(u, wv_ref[...])
    for j in range(N_PAIRS):
        src = slice(j * LANES, (j + 1) * LANES)
        chan = slice(j * PAIR_W, j * PAIR_W + LANES)
        bias = slice(j * PAIR_W + LANES, (j + 1) * PAIR_W)
        qx_ref[:, chan] = normed(q[:, src], gq).astype(BF16)
        qx_ref[:, bias] = (aug_q[:, src] + ones_q).astype(BF16)
        kx_ref[:, chan] = normed(k[:, src], gk).astype(BF16)
        kx_ref[:, bias] = (aug_k[:, src] + ones_k).astype(BF16)
        v_t = v[:, src].T
        for e in range(2):
            vt_ref[2 * j + e, :HEAD_DIM, :] = v_t[e * HEAD_DIM:(e + 1) * HEAD_DIM].astype(BF16)
            vt_ref[2 * j + e, HEAD_DIM:, :] = ones_row


def _fox_proj(h, g, wq, wk, wv, wf, bf, gq, gk, tri, selq, selk):
    out = jax.ShapeDtypeStruct((T, D_X), BF16)
    w_spec = _const_spec((D_MODEL, D_MODEL))
    sel_spec = _const_spec((N_SPLIT * LANES, D_MODEL))
    return pl.pallas_call(
        _fox_proj_body,
        name="fox_proj",
        grid=(T // TM,),
        in_specs=[_row_spec(D_MODEL), _const_spec((1, D_MODEL)), w_spec, w_spec, w_spec,
                  _const_spec((D_MODEL, LANES)), _const_spec((1, LANES)),
                  _const_spec((1, LANES)), _const_spec((1, LANES)),
                  _const_spec((TM, TM)), sel_spec, sel_spec],
        out_specs=[_row_spec(D_X), _row_spec(D_X),
                   pl.BlockSpec((None, N_HEADS, None, VT_ROWS, TM), lambda i: (i // NT, 0, i % NT, 0, 0))],
        out_shape=[out, out, jax.ShapeDtypeStruct((BATCH, N_HEADS, NT, VT_ROWS, TM), BF16)],
        scratch_shapes=[pltpu.VMEM((8, LANES), F32)],
        compiler_params=_params(("arbitrary",)),
    )(h, g.reshape(1, D_MODEL), wq, wk, wv, wf, bf, gq, gk, tri, selq, selk)


def _fox_attn_body(q_ref, k_ref, vt_ref, o_ref, s_ref, m_ref, acc_ref):
    i = pl.program_id(2)
    lane_x = lax.broadcasted_iota(jnp.int32, (1, PAIR_W), 1)
    heads = [(g, e) for g in range(ATTN_PAIRS) for e in range(2)]
    q_heads = []
    for g, e in heads:
        qx = q_ref[0, :, g * PAIR_W:(g + 1) * PAIR_W]
        q_heads.append(jnp.where(_head_lanes(lane_x, e), qx, jnp.zeros_like(qx)))

    def span(kb, n, ends_on_diagonal=False):
        keys = pl.ds(pl.multiple_of(kb * TM, TM), n * TM)
        for hd, (g, e) in enumerate(heads):
            k = k_ref[0, keys, g * PAIR_W:(g + 1) * PAIR_W]
            s = lax.dot_general(k, q_heads[hd], (((1,), (1,)), ((), ())), preferred_element_type=F32)
            if ends_on_diagonal:
                key = lax.broadcasted_iota(jnp.int32, (n * TM, TM), 0)
                qry = lax.broadcasted_iota(jnp.int32, (n * TM, TM), 1)
                s = jnp.where(key <= qry + (n - 1) * TM, s, MASK_VALUE)
            s_ref[hd, :n * TM, :] = s
        for hd in range(len(heads)):
            s = s_ref[hd, :n * TM, :]
            m_old = m_ref[hd]
            m_new = jnp.maximum(m_old, jnp.max(s, axis=0, keepdims=True))
            p = jnp.exp2(s - m_new).astype(BF16)
            m_ref[hd] = m_new
            v_t = jnp.concatenate([vt_ref[hd, kb + b] for b in range(n)], axis=1)
            acc_ref[hd] = jnp.exp2(m_old - m_new) * acc_ref[hd] + _dot(v_t, p)

    m_ref[...] = jnp.full(m_ref.shape, MASK_VALUE, F32)
    acc_ref[...] = jnp.zeros(acc_ref.shape, F32)

    @pl.when(i % 2 == 0)
    def _():
        span(i, 1, ends_on_diagonal=True)

    @pl.when(i % 2 == 1)
    def _():
        span(i - 1, 2, ends_on_diagonal=True)

    def step(u, carry):
        span(2 * u, 2)
        return carry

    lax.fori_loop(0, i // 2, step, 0)
    for g in range(ATTN_PAIRS):
        pair = []
        for e in range(2):
            acc = acc_ref[2 * g + e]
            pair.append(acc[:HEAD_DIM] * (1.0 / acc[HEAD_DIM:HEAD_DIM + 1]))
        o_ref[0, :, g * LANES:(g + 1) * LANES] = jnp.concatenate(pair, axis=0).T.astype(BF16)


def _fox_attn(qx, kx, vt):
    shape3 = (BATCH, LP, D_X)
    n_heads = 2 * ATTN_PAIRS
    q_spec = pl.BlockSpec((1, TM, ATTN_PAIRS * PAIR_W), lambda b, p, i: (b, i, p))
    k_spec = pl.BlockSpec((1, LP, ATTN_PAIRS * PAIR_W), lambda b, p, i: (b, 0, p))
    vt_spec = pl.BlockSpec((None, n_heads, NT, VT_ROWS, TM), lambda b, p, i: (b, p, 0, 0, 0))
    return pl.pallas_call(
        _fox_attn_body,
        name="fox_attn",
        grid=(BATCH, N_PAIRS // ATTN_PAIRS, NT),
        in_specs=[q_spec, k_spec, vt_spec],
        out_specs=pl.BlockSpec((1, TM, ATTN_PAIRS * LANES), lambda b, p, i: (b, i, p)),
        out_shape=jax.ShapeDtypeStruct((BATCH, LP, D_MODEL), BF16),
        scratch_shapes=[pltpu.VMEM((n_heads, 2 * TM, TM), F32),
                        pltpu.VMEM((n_heads, 1, TM), F32),
                        pltpu.VMEM((n_heads, VT_ROWS, TM), F32)],
        compiler_params=_params(("arbitrary", "arbitrary", "arbitrary")),
    )(qx.reshape(shape3), kx.reshape(shape3), vt).reshape(T, D_MODEL)


def _fox_out_body(h_ref, o_ref, w_ref, out_ref):
    out_ref[...] = h_ref[...] + _dot(o_ref[...], w_ref[...])


def _fox_out(h, o, w_out):
    return pl.pallas_call(
        _fox_out_body,
        name="fox_out",
        grid=(T // TM,),
        in_specs=[_row_spec(D_MODEL), _row_spec(D_MODEL), _const_spec((D_MODEL, D_MODEL))],
        out_specs=_row_spec(D_MODEL),
        out_shape=jax.ShapeDtypeStruct((T, D_MODEL), F32),
        compiler_params=_params(("arbitrary",)),
    )(h, o, w_out)


def _bias_selector(first_lane):
    r = jnp.arange(N_SPLIT * LANES)
    piece, head = r // LANES, r % LANES
    col = (head // 2) * LANES + (head % 2) * AUG_STRIDE + first_lane + piece
    hit = (jnp.arange(N_PAIRS * LANES)[None, :] == col[:, None]) & (head < N_HEADS)[:, None]
    return hit.astype(BF16)


def _fox(h, g, w_in, b_f, q_g, k_g, w_out):
    wq = w_in[:, :D_MODEL].astype(BF16)
    wk = w_in[:, D_MODEL:2 * D_MODEL].astype(BF16)
    wv = w_in[:, 2 * D_MODEL:3 * D_MODEL].astype(BF16)
    wf = jnp.pad(w_in[:, 3 * D_MODEL:], ((0, 0), (0, LANES - N_HEADS))).astype(BF16)
    bf = jnp.pad(b_f, (0, LANES - N_HEADS)).reshape(1, LANES)
    gq = jnp.tile(q_g, 2).reshape(1, LANES)
    gk = jnp.tile(k_g, 2).reshape(1, LANES)
    tri = (jnp.arange(TM)[None, :] <= jnp.arange(TM)[:, None]).astype(BF16)
    qx, kx, vt = _fox_proj(h, g, wq, wk, wv, wf, bf, gq, gk, tri,
                           _bias_selector(0), _bias_selector(N_SPLIT))
    return _fox_out(h, _fox_attn(qx, kx, vt), w_out.astype(BF16))


def kernel(x, meta, ffn_norm, ffn_w_gate, ffn_w_up, ffn_w_down, mix_norm, a_w_in, a_conv, a_w_out, b_w_in, b_conv, b_conv_bias, b_ln_g, b_ln_b, b_w_out, c_w_in, c_b_f, c_q_norm, c_k_norm, c_w_out):
    front = jnp.concatenate([jnp.zeros((PAD, D_MODEL), x.dtype), meta.astype(x.dtype)], axis=0)
    ffn_params = (ffn_norm.reshape(DEPTH, 2, 1, D_MODEL), ffn_w_gate.astype(BF16),
                  ffn_w_up.astype(BF16), ffn_w_down.astype(BF16))
    mixa_params = (mix_norm.reshape(DEPTH, 1, D_MODEL), a_w_in.astype(BF16), a_conv,
                   a_w_out.astype(BF16))
    h = None
    for i in range(DEPTH):
        m, j = i % N_MIXERS, i // N_MIXERS
        h = _ffn_first(x, front, (i, 0), *ffn_params) if i == 0 else _ffn(h, (i, 0), *ffn_params)
        if m == 0:
            h = _mixa(h, i, j, *mixa_params)
        elif m == 1:
            h = _mixb(h, mix_norm[i], b_w_in[j].astype(BF16), b_conv[j], b_conv_bias[j],
                      b_ln_g[j], b_ln_b[j], b_w_out[j].astype(BF16))
        else:
            h = _fox(h, mix_norm[i], c_w_in[j], c_b_f[j], c_q_norm[j], c_k_norm[j], c_w_out[j])
        h = _ffn_last(h, (i, 1), *ffn_params) if i == DEPTH - 1 else _ffn(h, (i, 1), *ffn_params)
    return h
```

```python
import jax
import jax.numpy as jnp
from jax import lax
from jax.experimental import pallas as pl
from jax.experimental.pallas import tpu as pltpu

D_MODEL = 1024
BATCH = 2
SEQ = 8192
DEPTH = 4
N_MIXERS = 3
N_META = 16
D_FF = 2816
SC_WIDTH = 3
CONF_WIDTH = 31
N_HEADS = 16
HEAD_DIM = D_MODEL // N_HEADS
EPS = 1e-6
MASK_VALUE = -1e30
LOG2E = 1.4426950408889634

LANES = 128
PAD = (-(N_META + SEQ)) % LANES
LP = PAD + N_META + SEQ
T = BATCH * LP
FRONT = PAD + N_META
TM = 640
NT = LP // TM
TM_LAST = 512
CAST_STEPS = 16
N_PAIRS = N_HEADS // 2
PAIR_W = 2 * LANES
D_X = N_PAIRS * PAIR_W
N_SPLIT = 3
AUG_STRIDE = 8
ATTN_PAIRS = 2
VT_ROWS = HEAD_DIM + 16
SC_HALO = 8
CONF_HALO = 32
CONF_ROWS = 64
VMEM_LIMIT = 56 * 1024 * 1024

F32 = jnp.float32
BF16 = jnp.bfloat16


def _rms(x, g):
    return x * lax.rsqrt(jnp.mean(x * x, axis=-1, keepdims=True) + EPS) * g


def _sigmoid(x):
    return 1.0 / (1.0 + jnp.exp(-x))


def _dot(a, b):
    return jnp.dot(a, b, preferred_element_type=F32)


def _split3(x):
    hi = x.astype(BF16)
    r1 = x - hi.astype(F32)
    mid = r1.astype(BF16)
    lo = (r1 - mid.astype(F32)).astype(BF16)
    return hi, mid, lo


def _row_in_batch(tile_in_batch):
    return tile_in_batch * TM + lax.broadcasted_iota(jnp.int32, (TM, 1), 0)


def _const_spec(shape):
    return pl.BlockSpec(shape, lambda *_: (0,) * len(shape), pipeline_mode=pl.Buffered(1))


def _picked_spec(pick, tail):
    return pl.BlockSpec((None,) * len(pick) + tail, lambda *_: tuple(pick) + (0,) * len(tail),
                        pipeline_mode=pl.Buffered(1))


def _row_spec(width):
    return pl.BlockSpec((TM, width), lambda i: (i, 0))


def _params(sem):
    return pltpu.CompilerParams(dimension_semantics=sem, vmem_limit_bytes=VMEM_LIMIT)


def _ffn_rows(h, g_ref, wg_ref, wu_ref, wd_ref):
    x = _rms(h, g_ref[...]).astype(BF16)
    gate = _dot(x, wg_ref[...])
    up = _dot(x, wu_ref[...])
    a = (gate * _sigmoid(gate) * up).astype(BF16)
    return h + 0.5 * _dot(a, wd_ref[...])


def _cast_next(refs):
    for src, dst in zip(refs[:3], refs[3:]):
        dst[...] = src[...].astype(BF16)


def _ffn_body(h_ref, g_ref, wg_ref, wu_ref, wd_ref, *rest):
    o_ref = rest[-1] if len(rest) == 1 else rest[3]
    o_ref[...] = _ffn_rows(h_ref[...], g_ref, wg_ref, wu_ref, wd_ref)
    if len(rest) > 1:
        _cast_next(rest[:3] + rest[4:])


def _ffn_first_body(x_ref, front_ref, g_ref, wg_ref, wu_ref, wd_ref, nwg_ref, nwu_ref, nwd_ref,
                    o_ref, cwg_ref, cwu_ref, cwd_ref):
    xb = x_ref[...]
    head = jnp.concatenate([front_ref[...], xb[:TM - FRONT]], axis=0)
    h = jnp.where(pl.program_id(1) == 0, head, xb)
    o_ref[...] = _ffn_rows(h, g_ref, wg_ref, wu_ref, wd_ref)
    _cast_next((nwg_ref, nwu_ref, nwd_ref, cwg_ref, cwu_ref, cwd_ref))


def _ffn_weight_specs(pick):
    return [_picked_spec(pick, (1, D_MODEL)), _const_spec((D_MODEL, D_FF)),
            _const_spec((D_MODEL, D_FF)), _const_spec((D_FF, D_MODEL))]


def _cast_specs(pick, step_of):
    def chunk(*ids):
        return jnp.minimum(step_of(*ids), CAST_STEPS - 1)

    shapes = [(D_MODEL, D_FF), (D_MODEL, D_FF), (D_FF, D_MODEL)]
    in_specs = [pl.BlockSpec((None, None, r // CAST_STEPS, c), lambda *ids: (*pick, chunk(*ids), 0))
                for r, c in shapes]
    out_specs = [pl.BlockSpec((r // CAST_STEPS, c), lambda *ids: (chunk(*ids), 0)) for r, c in shapes]
    out_shapes = [jax.ShapeDtypeStruct(sh, BF16) for sh in shapes]
    return in_specs, out_specs, out_shapes


def _ffn(h, pick, g, w, nxt, w_f32):
    c_in, c_out, c_shapes = _cast_specs(nxt, lambda i: i)
    out = pl.pallas_call(
        _ffn_body,
        name="ffn",
        grid=(T // TM,),
        in_specs=[_row_spec(D_MODEL)] + _ffn_weight_specs(pick) + c_in,
        out_specs=[_row_spec(D_MODEL)] + c_out,
        out_shape=[jax.ShapeDtypeStruct((T, D_MODEL), F32)] + c_shapes,
        compiler_params=_params(("arbitrary",)),
    )(h, g, *w, *w_f32)
    return out[0], tuple(out[1:])


def _ffn_first(x, front, pick, g, w, nxt, w_f32):
    x_spec = pl.BlockSpec(
        (pl.Element(TM), pl.Element(D_MODEL)),
        lambda b, t: (pl.multiple_of(b * SEQ + jnp.maximum(t * TM - FRONT, 0), LANES), 0))
    c_in, c_out, c_shapes = _cast_specs(nxt, lambda b, t: b * NT + t)
    out = pl.pallas_call(
        _ffn_first_body,
        name="ffn_first",
        grid=(BATCH, NT),
        in_specs=[x_spec, _const_spec((FRONT, D_MODEL))] + _ffn_weight_specs(pick) + c_in,
        out_specs=[pl.BlockSpec((TM, D_MODEL), lambda b, t: (b * NT + t, 0))] + c_out,
        out_shape=[jax.ShapeDtypeStruct((T, D_MODEL), F32)] + c_shapes,
        compiler_params=_params(("arbitrary", "arbitrary")),
    )(x.reshape(BATCH * SEQ, D_MODEL), front, g, *w, *w_f32)
    return out[0], tuple(out[1:])


def _ffn_last(h, pick, g, w):
    n_tiles = SEQ // TM_LAST
    h_spec = pl.BlockSpec(
        (pl.Element(TM_LAST), pl.Element(D_MODEL)),
        lambda b, t: (pl.multiple_of(b * LP + FRONT + t * TM_LAST, LANES), 0))
    return pl.pallas_call(
        _ffn_body,
        name="ffn_last",
        grid=(BATCH, n_tiles),
        in_specs=[h_spec] + _ffn_weight_specs(pick),
        out_specs=pl.BlockSpec((TM_LAST, D_MODEL), lambda b, t: (b * n_tiles + t, 0)),
        out_shape=jax.ShapeDtypeStruct((BATCH * SEQ, D_MODEL), F32),
        compiler_params=_params(("arbitrary", "arbitrary")),
    )(h, g, *w).reshape(BATCH, SEQ, D_MODEL)


def _mixa_body(h_ref, g_ref, win_ref, cw_ref, wout_ref, o_ref, buf_ref):
    t = pl.program_id(0) % NT

    @pl.when(t == 0)
    def _():
        buf_ref[0:SC_HALO, :] = jnp.zeros((SC_HALO, D_MODEL), F32)

    h = h_ref[...]
    u = _rms(h, g_ref[...]).astype(BF16)
    p = _dot(u, win_ref[...])
    b_gate = p[:, :D_MODEL]
    cv = p[:, D_MODEL:2 * D_MODEL] * p[:, 2 * D_MODEL:]
    cv = jnp.where(_row_in_batch(t) >= PAD, cv, 0.0)
    buf_ref[SC_HALO:SC_HALO + TM, :] = cv
    y = cw_ref[SC_WIDTH - 1:SC_WIDTH, :] * cv
    for k in range(SC_WIDTH - 1):
        shift = SC_WIDTH - 1 - k
        y = y + cw_ref[k:k + 1, :] * buf_ref[SC_HALO - shift:SC_HALO - shift + TM, :]
    z = (b_gate * y).astype(BF16)
    o_ref[...] = h + _dot(z, wout_ref[...])
    buf_ref[0:SC_HALO, :] = buf_ref[TM:TM + SC_HALO, :]


def _mixa(h, layer, j, g, w_in, conv_w, w_out):
    return pl.pallas_call(
        _mixa_body,
        name="short_conv_mixer",
        grid=(T // TM,),
        in_specs=[_row_spec(D_MODEL), _picked_spec((layer,), (1, D_MODEL)),
                  _picked_spec((j,), (D_MODEL, 3 * D_MODEL)), _picked_spec((j,), (SC_WIDTH, D_MODEL)),
                  _picked_spec((j,), (D_MODEL, D_MODEL))],
        out_specs=_row_spec(D_MODEL),
        out_shape=jax.ShapeDtypeStruct((T, D_MODEL), F32),
        scratch_shapes=[pltpu.VMEM((TM + SC_HALO, D_MODEL), F32)],
        compiler_params=_params(("arbitrary",)),
    )(h, g, w_in, conv_w, w_out)


def _mixb_body(h_ref, g_ref, win_ref, cw_ref, cb_ref, lng_ref, lnb_ref, wout_ref,
               o_ref, buf_ref, y_ref):
    t = pl.program_id(0) % NT

    @pl.when(t == 0)
    def _():
        buf_ref[:, 0:CONF_HALO, :] = jnp.zeros((D_MODEL // LANES, CONF_HALO, LANES), F32)

    h = h_ref[...]
    u = _rms(h, g_ref[...]).astype(BF16)
    real = _row_in_batch(t) >= PAD
    n_col = D_MODEL // LANES

    first = CONF_HALO - (CONF_WIDTH - 1)
    for c in range(n_col):
        p = _dot(u, win_ref[:, 2 * c * LANES:2 * (c + 1) * LANES])
        glu = jnp.where(real, p[:, :LANES] * _sigmoid(p[:, LANES:]), 0.0)
        buf_ref[c, CONF_HALO:CONF_HALO + TM, :] = glu
        w = cw_ref[c]
        bias = jnp.broadcast_to(cb_ref[c], (CONF_ROWS, LANES))
        for r in range(0, TM, CONF_ROWS):
            acc = bias
            for k in range(CONF_WIDTH):
                acc = acc + w[k:k + 1, :] * buf_ref[c, r + first + k:r + first + k + CONF_ROWS, :]
            y_ref[c, r:r + CONF_ROWS, :] = acc
        buf_ref[c, 0:CONF_HALO, :] = buf_ref[c, TM:TM + CONF_HALO, :]

    y = jnp.concatenate([y_ref[c] for c in range(n_col)], axis=1)
    mu = jnp.mean(y, axis=-1, keepdims=True)
    yc = y - mu
    var = jnp.mean(yc * yc, axis=-1, keepdims=True)
    ln = yc * lax.rsqrt(var + EPS) * lng_ref[...] + lnb_ref[...]
    s = (ln * _sigmoid(ln)).astype(BF16)
    o_ref[...] = h + _dot(s, wout_ref[...])


def _mixb(h, g, w_in, conv_w, conv_b, ln_g, ln_b, w_out):
    row = lambda v: v.reshape(1, D_MODEL)
    n_col = D_MODEL // LANES
    conv_w = conv_w.reshape(CONF_WIDTH, n_col, LANES).transpose(1, 0, 2)
    conv_b = conv_b.reshape(n_col, 1, LANES)
    w_in = w_in.reshape(D_MODEL, 2, n_col, LANES).transpose(0, 2, 1, 3).reshape(D_MODEL, 2 * D_MODEL)
    return pl.pallas_call(
        _mixb_body,
        name="conformer_mixer",
        grid=(T // TM,),
        in_specs=[_row_spec(D_MODEL), _const_spec((1, D_MODEL)),
                  _const_spec((D_MODEL, 2 * D_MODEL)), _const_spec((n_col, CONF_WIDTH, LANES)),
                  _const_spec((n_col, 1, LANES)), _const_spec((1, D_MODEL)), _const_spec((1, D_MODEL)),
                  _const_spec((D_MODEL, D_MODEL))],
        out_specs=_row_spec(D_MODEL),
        out_shape=jax.ShapeDtypeStruct((T, D_MODEL), F32),
        scratch_shapes=[pltpu.VMEM((n_col, TM + CONF_HALO, LANES), F32),
                        pltpu.VMEM((n_col, TM, LANES), F32)],
        compiler_params=_params(("arbitrary",)),
    )(h, row(g), w_in, conv_w, conv_b, row(ln_g), row(ln_b), w_out)


def _head_lanes(lane, e):
    chan = (lane >= e * HEAD_DIM) & (lane < (e + 1) * HEAD_DIM)
    a = LANES + AUG_STRIDE * e
    return chan | ((lane >= a) & (lane < a + 2 * N_SPLIT))


def _fox_proj_body(h_ref, g_ref, wq_ref, wk_ref, wv_ref, wf_ref, bf_ref, gq_ref, gk_ref,
                   tri_ref, selq_ref, selk_ref, qx_ref, kx_ref, vt_ref, carry_ref):
    t = pl.program_id(0) % NT

    @pl.when(t == 0)
    def _():
        carry_ref[...] = jnp.zeros_like(carry_ref)

    u = _rms(h_ref[...], g_ref[...]).astype(BF16)
    real = _row_in_batch(t) >= PAD

    f_logit = _dot(u, wf_ref[...]) + bf_ref[...]
    log_f = jnp.minimum(f_logit, 0.0) - jnp.log1p(jnp.exp(-jnp.abs(f_logit)))
    log_f = jnp.where(real, log_f, 0.0)
    c3 = _dot(tri_ref[...], jnp.concatenate(_split3(log_f), axis=1))
    cum = (c3[:, :LANES] + c3[:, LANES:2 * LANES] + c3[:, 2 * LANES:]) + carry_ref[0:1, :]
    carry_ref[...] = jnp.broadcast_to(cum[TM - 1:TM, :], carry_ref.shape)

    lane = lax.broadcasted_iota(jnp.int32, (1, LANES), 1)
    sub = lane % AUG_STRIDE
    in_aug = lane < 2 * AUG_STRIDE
    ones_q = (in_aug & (sub >= N_SPLIT) & (sub < 2 * N_SPLIT)).astype(F32)
    ones_k = (in_aug & (sub < N_SPLIT)).astype(F32)
    pad_rows = VT_ROWS - HEAD_DIM
    ones_row = (lax.broadcasted_iota(jnp.int32, (pad_rows, TM), 0) == 0).astype(BF16)

    neg_ck = jnp.where(real, -LOG2E * cum, MASK_VALUE)
    aug_q = _dot(jnp.concatenate(_split3(LOG2E * cum), axis=1), selq_ref[...])
    aug_k = _dot(jnp.concatenate(_split3(neg_ck), axis=1), selk_ref[...])

    first_head = lane < HEAD_DIM
    gq = gq_ref[...] * (LOG2E * HEAD_DIM ** -0.5)
    gk = gk_ref[...]

    def normed(x, gain):
        x2 = x * x
        ms0 = jnp.sum(jnp.where(first_head, x2, 0.0), axis=-1, keepdims=True)
        ms1 = jnp.sum(jnp.where(first_head, 0.0, x2), axis=-1, keepdims=True)
        ms = jnp.where(first_head, ms0, ms1) * (1.0 / HEAD_DIM)
        return x * lax.rsqrt(ms + EPS) * gain

    q = _dot(u, wq_ref[...])
    k = _dot(u, wk_ref[...])
    v = _dot(u, wv_ref[...])
    for j in range(N_PAIRS):
        src = slice(j * LANES, (j + 1) * LANES)
        chan = slice(j * PAIR_W, j * PAIR_W + LANES)
        bias = slice(j * PAIR_W + LANES, (j + 1) * PAIR_W)
        qx_ref[:, chan] = normed(q[:, src], gq).astype(BF16)
        qx_ref[:, bias] = (aug_q[:, src] + ones_q).astype(BF16)
        kx_ref[:, chan] = normed(k[:, src], gk).astype(BF16)
        kx_ref[:, bias] = (aug_k[:, src] + ones_k).astype(BF16)
        v_t = v[:, src].T
        for e in range(2):
            vt_ref[2 * j + e, :HEAD_DIM, :] = v_t[e * HEAD_DIM:(e + 1) * HEAD_DIM].astype(BF16)
            vt_ref[2 * j + e, HEAD_DIM:, :] = ones_row


def _fox_proj(h, g, wq, wk, wv, wf, bf, gq, gk, tri, selq, selk):
    out = jax.ShapeDtypeStruct((T, D_X), BF16)
    w_spec = _const_spec((D_MODEL, D_MODEL))
    sel_spec = _const_spec((N_SPLIT * LANES, D_MODEL))
    return pl.pallas_call(
        _fox_proj_body,
        name="fox_proj",
        grid=(T // TM,),
        in_specs=[_row_spec(D_MODEL), _const_spec((1, D_MODEL)), w_spec, w_spec, w_spec,
                  _const_spec((D_MODEL, LANES)), _const_spec((1, LANES)),
                  _const_spec((1, LANES)), _const_spec((1, LANES)),
                  _const_spec((TM, TM)), sel_spec, sel_spec],
        out_specs=[_row_spec(D_X), _row_spec(D_X),
                   pl.BlockSpec((None, N_HEADS, None, VT_ROWS, TM), lambda i: (i // NT, 0, i % NT, 0, 0))],
        out_shape=[out, out, jax.ShapeDtypeStruct((BATCH, N_HEADS, NT, VT_ROWS, TM), BF16)],
        scratch_shapes=[pltpu.VMEM((8, LANES), F32)],
        compiler_params=_params(("arbitrary",)),
    )(h, g.reshape(1, D_MODEL), wq, wk, wv, wf, bf, gq, gk, tri, selq, selk)


def _fox_attn_body(q_ref, k_ref, vt_ref, o_ref, s_ref, m_ref, acc_ref):
    i = pl.program_id(2)
    lane_x = lax.broadcasted_iota(jnp.int32, (1, PAIR_W), 1)
    heads = [(g, e) for g in range(ATTN_PAIRS) for e in range(2)]
    q_heads = []
    for g, e in heads:
        qx = q_ref[0, :, g * PAIR_W:(g + 1) * PAIR_W]
        q_heads.append(jnp.where(_head_lanes(lane_x, e), qx, jnp.zeros_like(qx)))

    def span(kb, n, ends_on_diagonal=False):
        keys = pl.ds(pl.multiple_of(kb * TM, TM), n * TM)
        for hd, (g, e) in enumerate(heads):
            k = k_ref[0, keys, g * PAIR_W:(g + 1) * PAIR_W]
            s = lax.dot_general(k, q_heads[hd], (((1,), (1,)), ((), ())), preferred_element_type=F32)
            if ends_on_diagonal:
                key = lax.broadcasted_iota(jnp.int32, (n * TM, TM), 0)
                qry = lax.broadcasted_iota(jnp.int32, (n * TM, TM), 1)
                s = jnp.where(key <= qry + (n - 1) * TM, s, MASK_VALUE)
            s_ref[hd, :n * TM, :] = s
        for hd in range(len(heads)):
            s = s_ref[hd, :n * TM, :]
            m_old = m_ref[hd]
            m_new = jnp.maximum(m_old, jnp.max(s, axis=0, keepdims=True))
            p = jnp.exp2(s - m_new).astype(BF16)
            m_ref[hd] = m_new
            v_t = jnp.concatenate([vt_ref[hd, kb + b] for b in range(n)], axis=1)
            acc_ref[hd] = jnp.exp2(m_old - m_new) * acc_ref[hd] + _dot(v_t, p)

    m_ref[...] = jnp.full(m_ref.shape, MASK_VALUE, F32)
    acc_ref[...] = jnp.zeros(acc_ref.shape, F32)

    @pl.when(i % 2 == 0)
    def _():
        span(i, 1, ends_on_diagonal=True)

    @pl.when(i % 2 == 1)
    def _():
        span(i - 1, 2, ends_on_diagonal=True)

    def step(u, carry):
        span(2 * u, 2)
        return carry

    lax.fori_loop(0, i // 2, step, 0)
    for g in range(ATTN_PAIRS):
        pair = []
        for e in range(2):
            acc = acc_ref[2 * g + e]
            pair.append(acc[:HEAD_DIM] * (1.0 / acc[HEAD_DIM:HEAD_DIM + 1]))
        o_ref[0, :, g * LANES:(g + 1) * LANES] = jnp.concatenate(pair, axis=0).T.astype(BF16)


def _fox_attn(qx, kx, vt):
    shape3 = (BATCH, LP, D_X)
    n_heads = 2 * ATTN_PAIRS
    q_spec = pl.BlockSpec((1, TM, ATTN_PAIRS * PAIR_W), lambda b, p, i: (b, i, p))
    k_spec = pl.BlockSpec((1, LP, ATTN_PAIRS * PAIR_W), lambda b, p, i: (b, 0, p))
    vt_spec = pl.BlockSpec((None, n_heads, NT, VT_ROWS, TM), lambda b, p, i: (b, p, 0, 0, 0))
    return pl.pallas_call(
        _fox_attn_body,
        name="fox_attn",
        grid=(BATCH, N_PAIRS // ATTN_PAIRS, NT),
        in_specs=[q_spec, k_spec, vt_spec],
        out_specs=pl.BlockSpec((1, TM, ATTN_PAIRS * LANES), lambda b, p, i: (b, i, p)),
        out_shape=jax.ShapeDtypeStruct((BATCH, LP, D_MODEL), BF16),
        scratch_shapes=[pltpu.VMEM((n_heads, 2 * TM, TM), F32),
                        pltpu.VMEM((n_heads, 1, TM), F32),
                        pltpu.VMEM((n_heads, VT_ROWS, TM), F32)],
        compiler_params=_params(("arbitrary", "arbitrary", "arbitrary")),
    )(qx.reshape(shape3), kx.reshape(shape3), vt).reshape(T, D_MODEL)


def _fox_out_body(h_ref, o_ref, w_ref, out_ref):
    out_ref[...] = h_ref[...] + _dot(o_ref[...], w_ref[...])


def _fox_out(h, o, w_out):
    return pl.pallas_call(
        _fox_out_body,
        name="fox_out",
        grid=(T // TM,),
        in_specs=[_row_spec(D_MODEL), _row_spec(D_MODEL), _const_spec((D_MODEL, D_MODEL))],
        out_specs=_row_spec(D_MODEL),
        out_shape=jax.ShapeDtypeStruct((T, D_MODEL), F32),
        compiler_params=_params(("arbitrary",)),
    )(h, o, w_out)


def _bias_selector(first_lane):
    r = jnp.arange(N_SPLIT * LANES)
    piece, head = r // LANES, r % LANES
    col = (head // 2) * LANES + (head % 2) * AUG_STRIDE + first_lane + piece
    hit = (jnp.arange(N_PAIRS * LANES)[None, :] == col[:, None]) & (head < N_HEADS)[:, None]
    return hit.astype(BF16)


def _fox(h, g, w_in, b_f, q_g, k_g, w_out):
    wq = w_in[:, :D_MODEL].astype(BF16)
    wk = w_in[:, D_MODEL:2 * D_MODEL].astype(BF16)
    wv = w_in[:, 2 * D_MODEL:3 * D_MODEL].astype(BF16)
    wf = jnp.pad(w_in[:, 3 * D_MODEL:], ((0, 0), (0, LANES - N_HEADS))).astype(BF16)
    bf = jnp.pad(b_f, (0, LANES - N_HEADS)).reshape(1, LANES)
    gq = jnp.tile(q_g, 2).reshape(1, LANES)
    gk = jnp.tile(k_g, 2).reshape(1, LANES)
    tri = (jnp.arange(TM)[None, :] <= jnp.arange(TM)[:, None]).astype(BF16)
    qx, kx, vt = _fox_proj(h, g, wq, wk, wv, wf, bf, gq, gk, tri,
                           _bias_selector(0), _bias_selector(N_SPLIT))
    return _fox_out(h, _fox_attn(qx, kx, vt), w_out.astype(BF16))


def kernel(x, meta, ffn_norm, ffn_w_gate, ffn_w_up, ffn_w_down, mix_norm, a_w_in, a_conv, a_w_out, b_w_in, b_conv, b_conv_bias, b_ln_g, b_ln_b, b_w_out, c_w_in, c_b_f, c_q_norm, c_k_norm, c_w_out):
    front = jnp.concatenate([jnp.zeros((PAD, D_MODEL), x.dtype), meta.astype(x.dtype)], axis=0)
    ffn_g = ffn_norm.reshape(DEPTH, 2, 1, D_MODEL)
    ffn_f32 = (ffn_w_gate, ffn_w_up, ffn_w_down)
    w = tuple(a[0, 0].astype(BF16) for a in ffn_f32)
    mixa_params = (mix_norm.reshape(DEPTH, 1, D_MODEL), a_w_in.astype(BF16), a_conv,
                   a_w_out.astype(BF16))
    h = None
    for i in range(DEPTH):
        m, j = i % N_MIXERS, i // N_MIXERS
        if i == 0:
            h, w = _ffn_first(x, front, (i, 0), ffn_g, w, (i, 1), ffn_f32)
        else:
            h, w = _ffn(h, (i, 0), ffn_g, w, (i, 1), ffn_f32)
        if m == 0:
            h = _mixa(h, i, j, *mixa_params)
        elif m == 1:
            h = _mixb(h, mix_norm[i], b_w_in[j].astype(BF16), b_conv[j], b_conv_bias[j],
                      b_ln_g[j], b_ln_b[j], b_w_out[j].astype(BF16))
        else:
            h = _fox(h, mix_norm[i], c_w_in[j], c_b_f[j], c_q_norm[j], c_k_norm[j], c_w_out[j])
        if i == DEPTH - 1:
            h = _ffn_last(h, (i, 1), ffn_g, w)
        else:
            h, w = _ffn(h, (i, 1), ffn_g, w, (i + 1, 0), ffn_f32)
    return h
```

```python
import jax
import jax.numpy as jnp
from jax import lax
from jax.experimental import pallas as pl
from jax.experimental.pallas import tpu as pltpu

D_MODEL = 1024
BATCH = 2
SEQ = 8192
DEPTH = 4
N_MIXERS = 3
N_META = 16
D_FF = 2816
SC_WIDTH = 3
CONF_WIDTH = 31
N_HEADS = 16
HEAD_DIM = D_MODEL // N_HEADS
EPS = 1e-6
MASK_VALUE = -1e30
LOG2E = 1.4426950408889634

LANES = 128
SUBLANES = 8
PAD = (-(N_META + SEQ)) % LANES
LP = PAD + N_META + SEQ
T = BATCH * LP
FRONT = PAD + N_META
TM = 640
NT = LP // TM
TM_FFN = 832
TM_LAST = 1024
CAST_STEPS = 16
N_PAIRS = N_HEADS // 2
PAIR_W = 2 * LANES
D_X = N_PAIRS * PAIR_W
N_SPLIT = 3
AUG_STRIDE = 8
ATTN_PAIRS = 2
VT_ROWS = HEAD_DIM + 16
SC_HALO = 8
CONF_HALO = 32
CONF_ROWS = 64
VMEM_LIMIT = 56 * 1024 * 1024

F32 = jnp.float32
BF16 = jnp.bfloat16


def _rms(x, g):
    return x * lax.rsqrt(jnp.mean(x * x, axis=-1, keepdims=True) + EPS) * g


def _sigmoid(x):
    return 1.0 / (1.0 + jnp.exp(-x))


def _dot(a, b):
    return jnp.dot(a, b, preferred_element_type=F32)


def _split3(x):
    hi = x.astype(BF16)
    r1 = x - hi.astype(F32)
    mid = r1.astype(BF16)
    lo = (r1 - mid.astype(F32)).astype(BF16)
    return hi, mid, lo


def _row_in_batch(tile_in_batch):
    return tile_in_batch * TM + lax.broadcasted_iota(jnp.int32, (TM, 1), 0)


def _const_spec(shape):
    return pl.BlockSpec(shape, lambda *_: (0,) * len(shape), pipeline_mode=pl.Buffered(1))


def _picked_spec(pick, tail):
    return pl.BlockSpec((None,) * len(pick) + tail, lambda *_: tuple(pick) + (0,) * len(tail),
                        pipeline_mode=pl.Buffered(1))


def _row_spec(width):
    return pl.BlockSpec((TM, width), lambda i: (i, 0))


def _params(sem):
    return pltpu.CompilerParams(dimension_semantics=sem, vmem_limit_bytes=VMEM_LIMIT)


def _ffn_rows(h, g_ref, wg_ref, wu_ref, wd_ref):
    x = _rms(h, g_ref[...]).astype(BF16)
    gate = _dot(x, wg_ref[...])
    up = _dot(x, wu_ref[...])
    a = (gate * _sigmoid(gate) * up).astype(BF16)
    return h + 0.5 * _dot(a, wd_ref[...])


def _cast_next(refs):
    for src, dst in zip(refs[:3], refs[3:]):
        dst[...] = src[...].astype(BF16)


def _ffn_body(h_ref, g_ref, wg_ref, wu_ref, wd_ref, *rest):
    o_ref = rest[-1] if len(rest) == 1 else rest[3]
    o_ref[...] = _ffn_rows(h_ref[...], g_ref, wg_ref, wu_ref, wd_ref)
    if len(rest) > 1:
        _cast_next(rest[:3] + rest[4:])


def _ffn_first_body(x_ref, front_ref, g_ref, wg_ref, wu_ref, wd_ref, nwg_ref, nwu_ref, nwd_ref,
                    o_ref, cwg_ref, cwu_ref, cwd_ref):
    xb = x_ref[...]
    head = jnp.concatenate([front_ref[...], xb[:TM_FFN - FRONT]], axis=0)
    h = jnp.where(pl.program_id(1) == 0, head, xb)
    o_ref[...] = _ffn_rows(h, g_ref, wg_ref, wu_ref, wd_ref)
    _cast_next((nwg_ref, nwu_ref, nwd_ref, cwg_ref, cwu_ref, cwd_ref))


def _ffn_weight_specs(pick):
    return [_picked_spec(pick, (1, D_MODEL)), _const_spec((D_MODEL, D_FF)),
            _const_spec((D_MODEL, D_FF)), _const_spec((D_FF, D_MODEL))]


def _cast_specs(pick, step_of):
    def chunk(*ids):
        return jnp.minimum(step_of(*ids), CAST_STEPS - 1)

    shapes = [(D_MODEL, D_FF), (D_MODEL, D_FF), (D_FF, D_MODEL)]
    in_specs = [pl.BlockSpec((None, None, r // CAST_STEPS, c), lambda *ids: (*pick, chunk(*ids), 0))
                for r, c in shapes]
    out_specs = [pl.BlockSpec((r // CAST_STEPS, c), lambda *ids: (chunk(*ids), 0)) for r, c in shapes]
    out_shapes = [jax.ShapeDtypeStruct(sh, BF16) for sh in shapes]
    return in_specs, out_specs, out_shapes


def _ffn(h, pick, g, w, nxt, w_f32):
    c_in, c_out, c_shapes = _cast_specs(nxt, lambda i: i)
    out = pl.pallas_call(
        _ffn_body,
        name="ffn",
        grid=(T // TM_FFN,),
        in_specs=[pl.BlockSpec((TM_FFN, D_MODEL), lambda i: (i, 0))] + _ffn_weight_specs(pick) + c_in,
        out_specs=[pl.BlockSpec((TM_FFN, D_MODEL), lambda i: (i, 0))] + c_out,
        out_shape=[jax.ShapeDtypeStruct((T, D_MODEL), F32)] + c_shapes,
        compiler_params=_params(("arbitrary",)),
    )(h, g, *w, *w_f32)
    return out[0], tuple(out[1:])


def _ffn_first(x, front, pick, g, w, nxt, w_f32):
    x_spec = pl.BlockSpec(
        (pl.Element(TM_FFN), pl.Element(D_MODEL)),
        lambda b, t: (pl.multiple_of(b * SEQ + jnp.maximum(t * TM_FFN - FRONT, 0), SUBLANES), 0))
    n_tiles = LP // TM_FFN
    c_in, c_out, c_shapes = _cast_specs(nxt, lambda b, t: b * n_tiles + t)
    out = pl.pallas_call(
        _ffn_first_body,
        name="ffn_first",
        grid=(BATCH, n_tiles),
        in_specs=[x_spec, _const_spec((FRONT, D_MODEL))] + _ffn_weight_specs(pick) + c_in,
        out_specs=[pl.BlockSpec((TM_FFN, D_MODEL), lambda b, t: (b * n_tiles + t, 0))] + c_out,
        out_shape=[jax.ShapeDtypeStruct((T, D_MODEL), F32)] + c_shapes,
        compiler_params=_params(("arbitrary", "arbitrary")),
    )(x.reshape(BATCH * SEQ, D_MODEL), front, g, *w, *w_f32)
    return out[0], tuple(out[1:])


def _ffn_last(h, pick, g, w):
    n_tiles = SEQ // TM_LAST
    h_spec = pl.BlockSpec(
        (pl.Element(TM_LAST), pl.Element(D_MODEL)),
        lambda b, t: (pl.multiple_of(b * LP + FRONT + t * TM_LAST, LANES), 0))
    return pl.pallas_call(
        _ffn_body,
        name="ffn_last",
        grid=(BATCH, n_tiles),
        in_specs=[h_spec] + _ffn_weight_specs(pick),
        out_specs=pl.BlockSpec((TM_LAST, D_MODEL), lambda b, t: (b * n_tiles + t, 0)),
        out_shape=jax.ShapeDtypeStruct((BATCH * SEQ, D_MODEL), F32),
        compiler_params=_params(("arbitrary", "arbitrary")),
    )(h, g, *w).reshape(BATCH, SEQ, D_MODEL)


def _mixa_body(h_ref, g_ref, win_ref, cw_ref, wout_ref, o_ref, buf_ref):
    t = pl.program_id(0) % NT

    @pl.when(t == 0)
    def _():
        buf_ref[0:SC_HALO, :] = jnp.zeros((SC_HALO, D_MODEL), F32)

    h = h_ref[...]
    u = _rms(h, g_ref[...]).astype(BF16)
    p = _dot(u, win_ref[...])
    b_gate = p[:, :D_MODEL]
    cv = p[:, D_MODEL:2 * D_MODEL] * p[:, 2 * D_MODEL:]
    cv = jnp.where(_row_in_batch(t) >= PAD, cv, 0.0)
    buf_ref[SC_HALO:SC_HALO + TM, :] = cv
    y = cw_ref[SC_WIDTH - 1:SC_WIDTH, :] * cv
    for k in range(SC_WIDTH - 1):
        shift = SC_WIDTH - 1 - k
        y = y + cw_ref[k:k + 1, :] * buf_ref[SC_HALO - shift:SC_HALO - shift + TM, :]
    z = (b_gate * y).astype(BF16)
    o_ref[...] = h + _dot(z, wout_ref[...])
    buf_ref[0:SC_HALO, :] = buf_ref[TM:TM + SC_HALO, :]


def _mixa(h, layer, j, g, w_in, conv_w, w_out):
    return pl.pallas_call(
        _mixa_body,
        name="short_conv_mixer",
        grid=(T // TM,),
        in_specs=[_row_spec(D_MODEL), _picked_spec((layer,), (1, D_MODEL)),
                  _picked_spec((j,), (D_MODEL, 3 * D_MODEL)), _picked_spec((j,), (SC_WIDTH, D_MODEL)),
                  _picked_spec((j,), (D_MODEL, D_MODEL))],
        out_specs=_row_spec(D_MODEL),
        out_shape=jax.ShapeDtypeStruct((T, D_MODEL), F32),
        scratch_shapes=[pltpu.VMEM((TM + SC_HALO, D_MODEL), F32)],
        compiler_params=_params(("arbitrary",)),
    )(h, g, w_in, conv_w, w_out)


def _mixb_body(h_ref, g_ref, win_ref, cw_ref, cb_ref, lng_ref, lnb_ref, wout_ref,
               o_ref, buf_ref, y_ref):
    t = pl.program_id(0) % NT

    @pl.when(t == 0)
    def _():
        buf_ref[:, 0:CONF_HALO, :] = jnp.zeros((D_MODEL // LANES, CONF_HALO, LANES), F32)

    h = h_ref[...]
    u = _rms(h, g_ref[...]).astype(BF16)
    real = _row_in_batch(t) >= PAD
    n_col = D_MODEL // LANES

    first = CONF_HALO - (CONF_WIDTH - 1)
    for c in range(n_col):
        p = _dot(u, win_ref[:, 2 * c * LANES:2 * (c + 1) * LANES])
        glu = jnp.where(real, p[:, :LANES] * _sigmoid(p[:, LANES:]), 0.0)
        buf_ref[c, CONF_HALO:CONF_HALO + TM, :] = glu
        w = cw_ref[c]
        bias = jnp.broadcast_to(cb_ref[c], (CONF_ROWS, LANES))
        for r in range(0, TM, CONF_ROWS):
            acc = bias
            for k in range(CONF_WIDTH):
                acc = acc + w[k:k + 1, :] * buf_ref[c, r + first + k:r + first + k + CONF_ROWS, :]
            y_ref[c, r:r + CONF_ROWS, :] = acc
        buf_ref[c, 0:CONF_HALO, :] = buf_ref[c, TM:TM + CONF_HALO, :]

    y = jnp.concatenate([y_ref[c] for c in range(n_col)], axis=1)
    mu = jnp.mean(y, axis=-1, keepdims=True)
    yc = y - mu
    var = jnp.mean(yc * yc, axis=-1, keepdims=True)
    ln = yc * lax.rsqrt(var + EPS) * lng_ref[...] + lnb_ref[...]
    s = (ln * _sigmoid(ln)).astype(BF16)
    o_ref[...] = h + _dot(s, wout_ref[...])


def _mixb(h, g, w_in, conv_w, conv_b, ln_g, ln_b, w_out):
    row = lambda v: v.reshape(1, D_MODEL)
    n_col = D_MODEL // LANES
    conv_w = conv_w.reshape(CONF_WIDTH, n_col, LANES).transpose(1, 0, 2)
    conv_b = conv_b.reshape(n_col, 1, LANES)
    w_in = w_in.reshape(D_MODEL, 2, n_col, LANES).transpose(0, 2, 1, 3).reshape(D_MODEL, 2 * D_MODEL)
    return pl.pallas_call(
        _mixb_body,
        name="conformer_mixer",
        grid=(T // TM,),
        in_specs=[_row_spec(D_MODEL), _const_spec((1, D_MODEL)),
                  _const_spec((D_MODEL, 2 * D_MODEL)), _const_spec((n_col, CONF_WIDTH, LANES)),
                  _const_spec((n_col, 1, LANES)), _const_spec((1, D_MODEL)), _const_spec((1, D_MODEL)),
                  _const_spec((D_MODEL, D_MODEL))],
        out_specs=_row_spec(D_MODEL),
        out_shape=jax.ShapeDtypeStruct((T, D_MODEL), F32),
        scratch_shapes=[pltpu.VMEM((n_col, TM + CONF_HALO, LANES), F32),
                        pltpu.VMEM((n_col, TM, LANES), F32)],
        compiler_params=_params(("arbitrary",)),
    )(h, row(g), w_in, conv_w, conv_b, row(ln_g), row(ln_b), w_out)


def _head_lanes(lane, e):
    chan = (lane >= e * HEAD_DIM) & (lane < (e + 1) * HEAD_DIM)
    a = LANES + AUG_STRIDE * e
    return chan | ((lane >= a) & (lane < a + 2 * N_SPLIT))


def _fox_proj_body(h_ref, g_ref, wq_ref, wk_ref, wv_ref, wf_ref, bf_ref, gq_ref, gk_ref,
                   tri_ref, selq_ref, selk_ref, qx_ref, kx_ref, vt_ref, carry_ref):
    t = pl.program_id(0) % NT

    @pl.when(t == 0)
    def _():
        carry_ref[...] = jnp.zeros_like(carry_ref)

    u = _rms(h_ref[...], g_ref[...]).astype(BF16)
    real = _row_in_batch(t) >= PAD

    f_logit = _dot(u, wf_ref[...]) + bf_ref[...]
    log_f = jnp.minimum(f_logit, 0.0) - jnp.log1p(jnp.exp(-jnp.abs(f_logit)))
    log_f = jnp.where(real, log_f, 0.0)
    c3 = _dot(tri_ref[...], jnp.concatenate(_split3(log_f), axis=1))
    cum = (c3[:, :LANES] + c3[:, LANES:2 * LANES] + c3[:, 2 * LANES:]) + carry_ref[0:1, :]
    carry_ref[...] = jnp.broadcast_to(cum[TM - 1:TM, :], carry_ref.shape)

    lane = lax.broadcasted_iota(jnp.int32, (1, LANES), 1)
    sub = lane % AUG_STRIDE
    in_aug = lane < 2 * AUG_STRIDE
    ones_q = (in_aug & (sub >= N_SPLIT) & (sub < 2 * N_SPLIT)).astype(F32)
    ones_k = (in_aug & (sub < N_SPLIT)).astype(F32)
    pad_rows = VT_ROWS - HEAD_DIM
    ones_row = (lax.broadcasted_iota(jnp.int32, (pad_rows, TM), 0) == 0).astype(BF16)

    neg_ck = jnp.where(real, -LOG2E * cum, MASK_VALUE)
    aug_q = _dot(jnp.concatenate(_split3(LOG2E * cum), axis=1), selq_ref[...])
    aug_k = _dot(jnp.concatenate(_split3(neg_ck), axis=1), selk_ref[...])

    first_head = lane < HEAD_DIM
    gq = gq_ref[...] * (LOG2E * HEAD_DIM ** -0.5)
    gk = gk_ref[...]

    def normed(x, gain):
        x2 = x * x
        ms0 = jnp.sum(jnp.where(first_head, x2, 0.0), axis=-1, keepdims=True)
        ms1 = jnp.sum(jnp.where(first_head, 0.0, x2), axis=-1, keepdims=True)
        ms = jnp.where(first_head, ms0, ms1) * (1.0 / HEAD_DIM)
        return x * lax.rsqrt(ms + EPS) * gain

    q = _dot(u, wq_ref[...])
    k = _dot(u, wk_ref[...])
    v = _dot(u, wv_ref[...])
    for j in range(N_PAIRS):
        src = slice(j * LANES, (j + 1) * LANES)
        chan = slice(j * PAIR_W, j * PAIR_W + LANES)
        bias = slice(j * PAIR_W + LANES, (j + 1) * PAIR_W)
        qx_ref[:, chan] = normed(q[:, src], gq).astype(BF16)
        qx_ref[:, bias] = (aug_q[:, src] + ones_q).astype(BF16)
        kx_ref[:, chan] = normed(k[:, src], gk).astype(BF16)
        kx_ref[:, bias] = (aug_k[:, src] + ones_k).astype(BF16)
        v_t = v[:, src].T
        for e in range(2):
            vt_ref[2 * j + e, :HEAD_DIM, :] = v_t[e * HEAD_DIM:(e + 1) * HEAD_DIM].astype(BF16)
            vt_ref[2 * j + e, HEAD_DIM:, :] = ones_row


def _fox_proj(h, g, wq, wk, wv, wf, bf, gq, gk, tri, selq, selk):
    out = jax.ShapeDtypeStruct((T, D_X), BF16)
    w_spec = _const_spec((D_MODEL, D_MODEL))
    sel_spec = _const_spec((N_SPLIT * LANES, D_MODEL))
    return pl.pallas_call(
        _fox_proj_body,
        name="fox_proj",
        grid=(T // TM,),
        in_specs=[_row_spec(D_MODEL), _const_spec((1, D_MODEL)), w_spec, w_spec, w_spec,
                  _const_spec((D_MODEL, LANES)), _const_spec((1, LANES)),
                  _const_spec((1, LANES)), _const_spec((1, LANES)),
                  _const_spec((TM, TM)), sel_spec, sel_spec],
        out_specs=[_row_spec(D_X), _row_spec(D_X),
                   pl.BlockSpec((None, N_HEADS, None, VT_ROWS, TM), lambda i: (i // NT, 0, i % NT, 0, 0))],
        out_shape=[out, out, jax.ShapeDtypeStruct((BATCH, N_HEADS, NT, VT_ROWS, TM), BF16)],
        scratch_shapes=[pltpu.VMEM((8, LANES), F32)],
        compiler_params=_params(("arbitrary",)),
    )(h, g.reshape(1, D_MODEL), wq, wk, wv, wf, bf, gq, gk, tri, selq, selk)


def _fox_attn_body(q_ref, k_ref, vt_ref, o_ref, s_ref, m_ref, acc_ref):
    i = pl.program_id(2)
    lane_x = lax.broadcasted_iota(jnp.int32, (1, PAIR_W), 1)
    heads = [(g, e) for g in range(ATTN_PAIRS) for e in range(2)]
    q_heads = []
    for g, e in heads:
        qx = q_ref[0, :, g * PAIR_W:(g + 1) * PAIR_W]
        q_heads.append(jnp.where(_head_lanes(lane_x, e), qx, jnp.zeros_like(qx)))

    def span(kb, n, ends_on_diagonal=False):
        keys = pl.ds(pl.multiple_of(kb * TM, TM), n * TM)
        for hd, (g, e) in enumerate(heads):
            k = k_ref[0, keys, g * PAIR_W:(g + 1) * PAIR_W]
            s = lax.dot_general(k, q_heads[hd], (((1,), (1,)), ((), ())), preferred_element_type=F32)
            if ends_on_diagonal:
                key = lax.broadcasted_iota(jnp.int32, (n * TM, TM), 0)
                qry = lax.broadcasted_iota(jnp.int32, (n * TM, TM), 1)
                s = jnp.where(key <= qry + (n - 1) * TM, s, MASK_VALUE)
            s_ref[hd, :n * TM, :] = s
        for hd in range(len(heads)):
            s = s_ref[hd, :n * TM, :]
            m_old = m_ref[hd]
            m_new = jnp.maximum(m_old, jnp.max(s, axis=0, keepdims=True))
            p = jnp.exp2(s - m_new).astype(BF16)
            m_ref[hd] = m_new
            v_t = jnp.concatenate([vt_ref[hd, kb + b] for b in range(n)], axis=1)
            acc_ref[hd] = jnp.exp2(m_old - m_new) * acc_ref[hd] + _dot(v_t, p)

    m_ref[...] = jnp.full(m_ref.shape, MASK_VALUE, F32)
    acc_ref[...] = jnp.zeros(acc_ref.shape, F32)

    @pl.when(i % 2 == 0)
    def _():
        span(i, 1, ends_on_diagonal=True)

    @pl.when(i % 2 == 1)
    def _():
        span(i - 1, 2, ends_on_diagonal=True)

    def step(u, carry):
        span(2 * u, 2)
        return carry

    lax.fori_loop(0, i // 2, step, 0)
    for g in range(ATTN_PAIRS):
        pair = []
        for e in range(2):
            acc = acc_ref[2 * g + e]
            pair.append(acc[:HEAD_DIM] * (1.0 / acc[HEAD_DIM:HEAD_DIM + 1]))
        o_ref[0, :, g * LANES:(g + 1) * LANES] = jnp.concatenate(pair, axis=0).T.astype(BF16)


def _fox_attn(qx, kx, vt):
    shape3 = (BATCH, LP, D_X)
    n_heads = 2 * ATTN_PAIRS
    q_spec = pl.BlockSpec((1, TM, ATTN_PAIRS * PAIR_W), lambda b, p, i: (b, i, p))
    k_spec = pl.BlockSpec((1, LP, ATTN_PAIRS * PAIR_W), lambda b, p, i: (b, 0, p))
    vt_spec = pl.BlockSpec((None, n_heads, NT, VT_ROWS, TM), lambda b, p, i: (b, p, 0, 0, 0))
    return pl.pallas_call(
        _fox_attn_body,
        name="fox_attn",
        grid=(BATCH, N_PAIRS // ATTN_PAIRS, NT),
        in_specs=[q_spec, k_spec, vt_spec],
        out_specs=pl.BlockSpec((1, TM, ATTN_PAIRS * LANES), lambda b, p, i: (b, i, p)),
        out_shape=jax.ShapeDtypeStruct((BATCH, LP, D_MODEL), BF16),
        scratch_shapes=[pltpu.VMEM((n_heads, 2 * TM, TM), F32),
                        pltpu.VMEM((n_heads, 1, TM), F32),
                        pltpu.VMEM((n_heads, VT_ROWS, TM), F32)],
        compiler_params=_params(("arbitrary", "arbitrary", "arbitrary")),
    )(qx.reshape(shape3), kx.reshape(shape3), vt).reshape(T, D_MODEL)


def _fox_out_body(h_ref, o_ref, w_ref, out_ref):
    out_ref[...] = h_ref[...] + _dot(o_ref[...], w_ref[...])


def _fox_out(h, o, w_out):
    return pl.pallas_call(
        _fox_out_body,
        name="fox_out",
        grid=(T // TM,),
        in_specs=[_row_spec(D_MODEL), _row_spec(D_MODEL), _const_spec((D_MODEL, D_MODEL))],
        out_specs=_row_spec(D_MODEL),
        out_shape=jax.ShapeDtypeStruct((T, D_MODEL), F32),
        compiler_params=_params(("arbitrary",)),
    )(h, o, w_out)


def _bias_selector(first_lane):
    r = jnp.arange(N_SPLIT * LANES)
    piece, head = r // LANES, r % LANES
    col = (head // 2) * LANES + (head % 2) * AUG_STRIDE + first_lane + piece
    hit = (jnp.arange(N_PAIRS * LANES)[None, :] == col[:, None]) & (head < N_HEADS)[:, None]
    return hit.astype(BF16)


def _fox(h, g, w_in, b_f, q_g, k_g, w_out):
    wq = w_in[:, :D_MODEL].astype(BF16)
    wk = w_in[:, D_MODEL:2 * D_MODEL].astype(BF16)
    wv = w_in[:, 2 * D_MODEL:3 * D_MODEL].astype(BF16)
    wf = jnp.pad(w_in[:, 3 * D_MODEL:], ((0, 0), (0, LANES - N_HEADS))).astype(BF16)
    bf = jnp.pad(b_f, (0, LANES - N_HEADS)).reshape(1, LANES)
    gq = jnp.tile(q_g, 2).reshape(1, LANES)
    gk = jnp.tile(k_g, 2).reshape(1, LANES)
    tri = (jnp.arange(TM)[None, :] <= jnp.arange(TM)[:, None]).astype(BF16)
    qx, kx, vt = _fox_proj(h, g, wq, wk, wv, wf, bf, gq, gk, tri,
                           _bias_selector(0), _bias_selector(N_SPLIT))
    return _fox_out(h, _fox_attn(qx, kx, vt), w_out.astype(BF16))


def kernel(x, meta, ffn_norm, ffn_w_gate, ffn_w_up, ffn_w_down, mix_norm, a_w_in, a_conv, a_w_out, b_w_in, b_conv, b_conv_bias, b_ln_g, b_ln_b, b_w_out, c_w_in, c_b_f, c_q_norm, c_k_norm, c_w_out):
    front = jnp.concatenate([jnp.zeros((PAD, D_MODEL), x.dtype), meta.astype(x.dtype)], axis=0)
    ffn_g = ffn_norm.reshape(DEPTH, 2, 1, D_MODEL)
    ffn_f32 = (ffn_w_gate, ffn_w_up, ffn_w_down)
    w = tuple(a[0, 0].astype(BF16) for a in ffn_f32)
    mixa_params = (mix_norm.reshape(DEPTH, 1, D_MODEL), a_w_in.astype(BF16), a_conv,
                   a_w_out.astype(BF16))
    h = None
    for i in range(DEPTH):
        m, j = i % N_MIXERS, i // N_MIXERS
        if i == 0:
            h, w = _ffn_first(x, front, (i, 0), ffn_g, w, (i, 1), ffn_f32)
        else:
            h, w = _ffn(h, (i, 0), ffn_g, w, (i, 1), ffn_f32)
        if m == 0:
            h = _mixa(h, i, j, *mixa_params)
        elif m == 1:
            h = _mixb(h, mix_norm[i], b_w_in[j].astype(BF16), b_conv[j], b_conv_bias[j],
                      b_ln_g[j], b_ln_b[j], b_w_out[j].astype(BF16))
        else:
            h = _fox(h, mix_norm[i], c_w_in[j], c_b_f[j], c_q_norm[j], c_k_norm[j], c_w_out[j])
        if i == DEPTH - 1:
            h = _ffn_last(h, (i, 1), ffn_g, w)
        else:
            h, w = _ffn(h, (i, 1), ffn_g, w, (i + 1, 0), ffn_f32)
    return h
```

```python
import jax
import jax.numpy as jnp
from jax import lax
from jax.experimental import pallas as pl
from jax.experimental.pallas import tpu as pltpu

D_MODEL = 1024
BATCH = 2
SEQ = 8192
DEPTH = 4
N_MIXERS = 3
N_META = 16
D_FF = 2816
SC_WIDTH = 3
CONF_WIDTH = 31
N_HEADS = 16
HEAD_DIM = D_MODEL // N_HEADS
EPS = 1e-6
MASK_VALUE = -1e30
LOG2E = 1.4426950408889634

LANES = 128
SUBLANES = 8
PAD = (-(N_META + SEQ)) % LANES
LP = PAD + N_META + SEQ
T = BATCH * LP
FRONT = PAD + N_META
TM = 640
NT = LP // TM
TM_FFN = 832
TM_LAST = 1024
CAST_STEPS = 16
N_PAIRS = N_HEADS // 2
PAIR_W = 2 * LANES
D_X = N_PAIRS * PAIR_W
N_SPLIT = 3
AUG_STRIDE = 8
ATTN_PAIRS = 2
VT_ROWS = HEAD_DIM + 16
SC_HALO = 8
CONF_HALO = 32
CONF_ROWS = 64
VMEM_LIMIT = 56 * 1024 * 1024

F32 = jnp.float32
BF16 = jnp.bfloat16


def _rms(x, g):
    return x * lax.rsqrt(jnp.mean(x * x, axis=-1, keepdims=True) + EPS) * g


def _sigmoid(x):
    return 1.0 / (1.0 + jnp.exp(-x))


def _dot(a, b):
    return jnp.dot(a, b, preferred_element_type=F32)


def _split3(x):
    hi = x.astype(BF16)
    r1 = x - hi.astype(F32)
    mid = r1.astype(BF16)
    lo = (r1 - mid.astype(F32)).astype(BF16)
    return hi, mid, lo


def _row_in_batch(tile_in_batch):
    return tile_in_batch * TM + lax.broadcasted_iota(jnp.int32, (TM, 1), 0)


def _const_spec(shape):
    return pl.BlockSpec(shape, lambda *_: (0,) * len(shape), pipeline_mode=pl.Buffered(1))


def _picked_spec(pick, tail):
    return pl.BlockSpec((None,) * len(pick) + tail, lambda *_: tuple(pick) + (0,) * len(tail),
                        pipeline_mode=pl.Buffered(1))


def _row_spec(width):
    return pl.BlockSpec((TM, width), lambda i: (i, 0))


def _params(sem):
    return pltpu.CompilerParams(dimension_semantics=sem, vmem_limit_bytes=VMEM_LIMIT)


def _ffn_rows(h, g_ref, wg_ref, wu_ref, wd_ref):
    x = _rms(h, g_ref[...]).astype(BF16)
    gate = _dot(x, wg_ref[...])
    up = _dot(x, wu_ref[...])
    a = (gate * _sigmoid(gate) * up).astype(BF16)
    return h + 0.5 * _dot(a, wd_ref[...])


def _cast_next(refs):
    for src, dst in zip(refs[:3], refs[3:]):
        dst[...] = src[...].astype(BF16)


def _ffn_body(h_ref, g_ref, wg_ref, wu_ref, wd_ref, *rest):
    o_ref = rest[-1] if len(rest) == 1 else rest[3]
    o_ref[...] = _ffn_rows(h_ref[...], g_ref, wg_ref, wu_ref, wd_ref)
    if len(rest) > 1:
        _cast_next(rest[:3] + rest[4:])


def _ffn_after_attn_body(h_ref, o_ref, wo_ref, g_ref, wg_ref, wu_ref, wd_ref, *rest):
    h = h_ref[...] + _dot(o_ref[...], wo_ref[...])
    rest[3][...] = _ffn_rows(h, g_ref, wg_ref, wu_ref, wd_ref)
    _cast_next(rest[:3] + rest[4:])


def _ffn_first_body(x_ref, front_ref, g_ref, wg_ref, wu_ref, wd_ref, nwg_ref, nwu_ref, nwd_ref,
                    o_ref, cwg_ref, cwu_ref, cwd_ref):
    xb = x_ref[...]
    head = jnp.concatenate([front_ref[...], xb[:TM_FFN - FRONT]], axis=0)
    h = jnp.where(pl.program_id(1) == 0, head, xb)
    o_ref[...] = _ffn_rows(h, g_ref, wg_ref, wu_ref, wd_ref)
    _cast_next((nwg_ref, nwu_ref, nwd_ref, cwg_ref, cwu_ref, cwd_ref))


def _ffn_weight_specs(pick):
    return [_picked_spec(pick, (1, D_MODEL)), _const_spec((D_MODEL, D_FF)),
            _const_spec((D_MODEL, D_FF)), _const_spec((D_FF, D_MODEL))]


def _cast_specs(pick, step_of):
    def chunk(*ids):
        return jnp.minimum(step_of(*ids), CAST_STEPS - 1)

    shapes = [(D_MODEL, D_FF), (D_MODEL, D_FF), (D_FF, D_MODEL)]
    in_specs = [pl.BlockSpec((None, None, r // CAST_STEPS, c), lambda *ids: (*pick, chunk(*ids), 0))
                for r, c in shapes]
    out_specs = [pl.BlockSpec((r // CAST_STEPS, c), lambda *ids: (chunk(*ids), 0)) for r, c in shapes]
    out_shapes = [jax.ShapeDtypeStruct(sh, BF16) for sh in shapes]
    return in_specs, out_specs, out_shapes


def _ffn(h, pick, g, w, nxt, w_f32):
    c_in, c_out, c_shapes = _cast_specs(nxt, lambda i: i)
    out = pl.pallas_call(
        _ffn_body,
        name="ffn",
        grid=(T // TM_FFN,),
        in_specs=[pl.BlockSpec((TM_FFN, D_MODEL), lambda i: (i, 0))] + _ffn_weight_specs(pick) + c_in,
        out_specs=[pl.BlockSpec((TM_FFN, D_MODEL), lambda i: (i, 0))] + c_out,
        out_shape=[jax.ShapeDtypeStruct((T, D_MODEL), F32)] + c_shapes,
        compiler_params=_params(("arbitrary",)),
    )(h, g, *w, *w_f32)
    return out[0], tuple(out[1:])


def _ffn_after_attn(h, o, w_out, pick, g, w, nxt, w_f32):
    tile = pl.BlockSpec((TM_FFN, D_MODEL), lambda i: (i, 0))
    c_in, c_out, c_shapes = _cast_specs(nxt, lambda i: i)
    out = pl.pallas_call(
        _ffn_after_attn_body,
        name="ffn_after_attn",
        grid=(T // TM_FFN,),
        in_specs=[tile, tile, _const_spec((D_MODEL, D_MODEL))] + _ffn_weight_specs(pick) + c_in,
        out_specs=[tile] + c_out,
        out_shape=[jax.ShapeDtypeStruct((T, D_MODEL), F32)] + c_shapes,
        compiler_params=_params(("arbitrary",)),
    )(h, o, w_out, g, *w, *w_f32)
    return out[0], tuple(out[1:])


def _ffn_first(x, front, pick, g, w, nxt, w_f32):
    x_spec = pl.BlockSpec(
        (pl.Element(TM_FFN), pl.Element(D_MODEL)),
        lambda b, t: (pl.multiple_of(b * SEQ + jnp.maximum(t * TM_FFN - FRONT, 0), SUBLANES), 0))
    n_tiles = LP // TM_FFN
    c_in, c_out, c_shapes = _cast_specs(nxt, lambda b, t: b * n_tiles + t)
    out = pl.pallas_call(
        _ffn_first_body,
        name="ffn_first",
        grid=(BATCH, n_tiles),
        in_specs=[x_spec, _const_spec((FRONT, D_MODEL))] + _ffn_weight_specs(pick) + c_in,
        out_specs=[pl.BlockSpec((TM_FFN, D_MODEL), lambda b, t: (b * n_tiles + t, 0))] + c_out,
        out_shape=[jax.ShapeDtypeStruct((T, D_MODEL), F32)] + c_shapes,
        compiler_params=_params(("arbitrary", "arbitrary")),
    )(x.reshape(BATCH * SEQ, D_MODEL), front, g, *w, *w_f32)
    return out[0], tuple(out[1:])


def _ffn_last(h, pick, g, w):
    n_tiles = SEQ // TM_LAST
    h_spec = pl.BlockSpec(
        (pl.Element(TM_LAST), pl.Element(D_MODEL)),
        lambda b, t: (pl.multiple_of(b * LP + FRONT + t * TM_LAST, LANES), 0))
    return pl.pallas_call(
        _ffn_body,
        name="ffn_last",
        grid=(BATCH, n_tiles),
        in_specs=[h_spec] + _ffn_weight_specs(pick),
        out_specs=pl.BlockSpec((TM_LAST, D_MODEL), lambda b, t: (b * n_tiles + t, 0)),
        out_shape=jax.ShapeDtypeStruct((BATCH * SEQ, D_MODEL), F32),
        compiler_params=_params(("arbitrary", "arbitrary")),
    )(h, g, *w).reshape(BATCH, SEQ, D_MODEL)


def _mixa_body(h_ref, g_ref, win_ref, cw_ref, wout_ref, o_ref, buf_ref):
    t = pl.program_id(0) % NT

    @pl.when(t == 0)
    def _():
        buf_ref[0:SC_HALO, :] = jnp.zeros((SC_HALO, D_MODEL), F32)

    h = h_ref[...]
    u = _rms(h, g_ref[...]).astype(BF16)
    p = _dot(u, win_ref[...])
    b_gate = p[:, :D_MODEL]
    cv = p[:, D_MODEL:2 * D_MODEL] * p[:, 2 * D_MODEL:]
    cv = jnp.where(_row_in_batch(t) >= PAD, cv, 0.0)
    buf_ref[SC_HALO:SC_HALO + TM, :] = cv
    y = cw_ref[SC_WIDTH - 1:SC_WIDTH, :] * cv
    for k in range(SC_WIDTH - 1):
        shift = SC_WIDTH - 1 - k
        y = y + cw_ref[k:k + 1, :] * buf_ref[SC_HALO - shift:SC_HALO - shift + TM, :]
    z = (b_gate * y).astype(BF16)
    o_ref[...] = h + _dot(z, wout_ref[...])
    buf_ref[0:SC_HALO, :] = buf_ref[TM:TM + SC_HALO, :]


def _mixa(h, layer, j, g, w_in, conv_w, w_out):
    return pl.pallas_call(
        _mixa_body,
        name="short_conv_mixer",
        grid=(T // TM,),
        in_specs=[_row_spec(D_MODEL), _picked_spec((layer,), (1, D_MODEL)),
                  _picked_spec((j,), (D_MODEL, 3 * D_MODEL)), _picked_spec((j,), (SC_WIDTH, D_MODEL)),
                  _picked_spec((j,), (D_MODEL, D_MODEL))],
        out_specs=_row_spec(D_MODEL),
        out_shape=jax.ShapeDtypeStruct((T, D_MODEL), F32),
        scratch_shapes=[pltpu.VMEM((TM + SC_HALO, D_MODEL), F32)],
        compiler_params=_params(("arbitrary",)),
    )(h, g, w_in, conv_w, w_out)


def _mixb_body(h_ref, g_ref, win_ref, cw_ref, cb_ref, lng_ref, lnb_ref, wout_ref,
               o_ref, buf_ref, y_ref):
    t = pl.program_id(0) % NT

    @pl.when(t == 0)
    def _():
        buf_ref[:, 0:CONF_HALO, :] = jnp.zeros((D_MODEL // LANES, CONF_HALO, LANES), F32)

    h = h_ref[...]
    u = _rms(h, g_ref[...]).astype(BF16)
    real = _row_in_batch(t) >= PAD
    n_col = D_MODEL // LANES

    first = CONF_HALO - (CONF_WIDTH - 1)
    for c in range(n_col):
        p = _dot(u, win_ref[:, 2 * c * LANES:2 * (c + 1) * LANES])
        glu = jnp.where(real, p[:, :LANES] * _sigmoid(p[:, LANES:]), 0.0)
        buf_ref[c, CONF_HALO:CONF_HALO + TM, :] = glu
        w = cw_ref[c]
        bias = jnp.broadcast_to(cb_ref[c], (CONF_ROWS, LANES))
        for r in range(0, TM, CONF_ROWS):
            acc = bias
            for k in range(CONF_WIDTH):
                acc = acc + w[k:k + 1, :] * buf_ref[c, r + first + k:r + first + k + CONF_ROWS, :]
            y_ref[c, r:r + CONF_ROWS, :] = acc
        buf_ref[c, 0:CONF_HALO, :] = buf_ref[c, TM:TM + CONF_HALO, :]

    y = jnp.concatenate([y_ref[c] for c in range(n_col)], axis=1)
    mu = jnp.mean(y, axis=-1, keepdims=True)
    yc = y - mu
    var = jnp.mean(yc * yc, axis=-1, keepdims=True)
    ln = yc * lax.rsqrt(var + EPS) * lng_ref[...] + lnb_ref[...]
    s = (ln * _sigmoid(ln)).astype(BF16)
    o_ref[...] = h + _dot(s, wout_ref[...])


def _mixb(h, g, w_in, conv_w, conv_b, ln_g, ln_b, w_out):
    row = lambda v: v.reshape(1, D_MODEL)
    n_col = D_MODEL // LANES
    conv_w = conv_w.reshape(CONF_WIDTH, n_col, LANES).transpose(1, 0, 2)
    conv_b = conv_b.reshape(n_col, 1, LANES)
    w_in = w_in.reshape(D_MODEL, 2, n_col, LANES).transpose(0, 2, 1, 3).reshape(D_MODEL, 2 * D_MODEL)
    return pl.pallas_call(
        _mixb_body,
        name="conformer_mixer",
        grid=(T // TM,),
        in_specs=[_row_spec(D_MODEL), _const_spec((1, D_MODEL)),
                  _const_spec((D_MODEL, 2 * D_MODEL)), _const_spec((n_col, CONF_WIDTH, LANES)),
                  _const_spec((n_col, 1, LANES)), _const_spec((1, D_MODEL)), _const_spec((1, D_MODEL)),
                  _const_spec((D_MODEL, D_MODEL))],
        out_specs=_row_spec(D_MODEL),
        out_shape=jax.ShapeDtypeStruct((T, D_MODEL), F32),
        scratch_shapes=[pltpu.VMEM((n_col, TM + CONF_HALO, LANES), F32),
                        pltpu.VMEM((n_col, TM, LANES), F32)],
        compiler_params=_params(("arbitrary",)),
    )(h, row(g), w_in, conv_w, conv_b, row(ln_g), row(ln_b), w_out)


def _head_lanes(lane, e):
    chan = (lane >= e * HEAD_DIM) & (lane < (e + 1) * HEAD_DIM)
    a = LANES + AUG_STRIDE * e
    return chan | ((lane >= a) & (lane < a + 2 * N_SPLIT))


def _fox_proj_body(h_ref, g_ref, wq_ref, wk_ref, wv_ref, wf_ref, bf_ref, gq_ref, gk_ref,
                   tri_ref, selq_ref, selk_ref, qx_ref, kx_ref, vt_ref, carry_ref):
    t = pl.program_id(0) % NT

    @pl.when(t == 0)
    def _():
        carry_ref[...] = jnp.zeros_like(carry_ref)

    u = _rms(h_ref[...], g_ref[...]).astype(BF16)
    real = _row_in_batch(t) >= PAD

    f_logit = _dot(u, wf_ref[...]) + bf_ref[...]
    log_f = jnp.minimum(f_logit, 0.0) - jnp.log1p(jnp.exp(-jnp.abs(f_logit)))
    log_f = jnp.where(real, log_f, 0.0)
    c3 = _dot(tri_ref[...], jnp.concatenate(_split3(log_f), axis=1))
    cum = (c3[:, :LANES] + c3[:, LANES:2 * LANES] + c3[:, 2 * LANES:]) + carry_ref[0:1, :]
    carry_ref[...] = jnp.broadcast_to(cum[TM - 1:TM, :], carry_ref.shape)

    lane = lax.broadcasted_iota(jnp.int32, (1, LANES), 1)
    sub = lane % AUG_STRIDE
    in_aug = lane < 2 * AUG_STRIDE
    ones_q = (in_aug & (sub >= N_SPLIT) & (sub < 2 * N_SPLIT)).astype(F32)
    ones_k = (in_aug & (sub < N_SPLIT)).astype(F32)
    pad_rows = VT_ROWS - HEAD_DIM
    ones_row = (lax.broadcasted_iota(jnp.int32, (pad_rows, TM), 0) == 0).astype(BF16)

    neg_ck = jnp.where(real, -LOG2E * cum, MASK_VALUE)
    aug_q = _dot(jnp.concatenate(_split3(LOG2E * cum), axis=1), selq_ref[...])
    aug_k = _dot(jnp.concatenate(_split3(neg_ck), axis=1), selk_ref[...])

    first_head = lane < HEAD_DIM
    gq = gq_ref[...] * (LOG2E * HEAD_DIM ** -0.5)
    gk = gk_ref[...]

    def normed(x, gain):
        x2 = x * x
        ms0 = jnp.sum(jnp.where(first_head, x2, 0.0), axis=-1, keepdims=True)
        ms1 = jnp.sum(jnp.where(first_head, 0.0, x2), axis=-1, keepdims=True)
        ms = jnp.where(first_head, ms0, ms1) * (1.0 / HEAD_DIM)
        return x * lax.rsqrt(ms + EPS) * gain

    q = _dot(u, wq_ref[...])
    k = _dot(u, wk_ref[...])
    v = _dot(u, wv_ref[...])
    for j in range(N_PAIRS):
        src = slice(j * LANES, (j + 1) * LANES)
        chan = slice(j * PAIR_W, j * PAIR_W + LANES)
        bias = slice(j * PAIR_W + LANES, (j + 1) * PAIR_W)
        qx_ref[:, chan] = normed(q[:, src], gq).astype(BF16)
        qx_ref[:, bias] = (aug_q[:, src] + ones_q).astype(BF16)
        kx_ref[:, chan] = normed(k[:, src], gk).astype(BF16)
        kx_ref[:, bias] = (aug_k[:, src] + ones_k).astype(BF16)
        v_t = v[:, src].T
        for e in range(2):
            vt_ref[2 * j + e, :HEAD_DIM, :] = v_t[e * HEAD_DIM:(e + 1) * HEAD_DIM].astype(BF16)
            vt_ref[2 * j + e, HEAD_DIM:, :] = ones_row


def _fox_proj(h, g, wq, wk, wv, wf, bf, gq, gk, tri, selq, selk):
    out = jax.ShapeDtypeStruct((T, D_X), BF16)
    w_spec = _const_spec((D_MODEL, D_MODEL))
    sel_spec = _const_spec((N_SPLIT * LANES, D_MODEL))
    return pl.pallas_call(
        _fox_proj_body,
        name="fox_proj",
        grid=(T // TM,),
        in_specs=[_row_spec(D_MODEL), _const_spec((1, D_MODEL)), w_spec, w_spec, w_spec,
                  _const_spec((D_MODEL, LANES)), _const_spec((1, LANES)),
                  _const_spec((1, LANES)), _const_spec((1, LANES)),
                  _const_spec((TM, TM)), sel_spec, sel_spec],
        out_specs=[_row_spec(D_X), _row_spec(D_X),
                   pl.BlockSpec((None, N_HEADS, None, VT_ROWS, TM), lambda i: (i // NT, 0, i % NT, 0, 0))],
        out_shape=[out, out, jax.ShapeDtypeStruct((BATCH, N_HEADS, NT, VT_ROWS, TM), BF16)],
        scratch_shapes=[pltpu.VMEM((8, LANES), F32)],
        compiler_params=_params(("arbitrary",)),
    )(h, g.reshape(1, D_MODEL), wq, wk, wv, wf, bf, gq, gk, tri, selq, selk)


def _fox_attn_body(q_ref, k_ref, vt_ref, o_ref, s_ref, m_ref, acc_ref):
    i = pl.program_id(2)
    lane_x = lax.broadcasted_iota(jnp.int32, (1, PAIR_W), 1)
    heads = [(g, e) for g in range(ATTN_PAIRS) for e in range(2)]
    q_heads = []
    for g, e in heads:
        qx = q_ref[0, :, g * PAIR_W:(g + 1) * PAIR_W]
        q_heads.append(jnp.where(_head_lanes(lane_x, e), qx, jnp.zeros_like(qx)))

    def span(kb, n, ends_on_diagonal=False):
        keys = pl.ds(pl.multiple_of(kb * TM, TM), n * TM)
        for hd, (g, e) in enumerate(heads):
            k = k_ref[0, keys, g * PAIR_W:(g + 1) * PAIR_W]
            s = lax.dot_general(k, q_heads[hd], (((1,), (1,)), ((), ())), preferred_element_type=F32)
            if ends_on_diagonal:
                key = lax.broadcasted_iota(jnp.int32, (n * TM, TM), 0)
                qry = lax.broadcasted_iota(jnp.int32, (n * TM, TM), 1)
                s = jnp.where(key <= qry + (n - 1) * TM, s, MASK_VALUE)
            s_ref[hd, :n * TM, :] = s
        for hd in range(len(heads)):
            s = s_ref[hd, :n * TM, :]
            m_old = m_ref[hd]
            m_new = jnp.maximum(m_old, jnp.max(s, axis=0, keepdims=True))
            p = jnp.exp2(s - m_new).astype(BF16)
            m_ref[hd] = m_new
            v_t = jnp.concatenate([vt_ref[hd, kb + b] for b in range(n)], axis=1)
            acc_ref[hd] = jnp.exp2(m_old - m_new) * acc_ref[hd] + _dot(v_t, p)

    m_ref[...] = jnp.full(m_ref.shape, MASK_VALUE, F32)
    acc_ref[...] = jnp.zeros(acc_ref.shape, F32)

    @pl.when(i % 2 == 0)
    def _():
        span(i, 1, ends_on_diagonal=True)

    @pl.when(i % 2 == 1)
    def _():
        span(i - 1, 2, ends_on_diagonal=True)

    def step(u, carry):
        span(2 * u, 2)
        return carry

    lax.fori_loop(0, i // 2, step, 0)
    for g in range(ATTN_PAIRS):
        pair = []
        for e in range(2):
            acc = acc_ref[2 * g + e]
            pair.append(acc[:HEAD_DIM] * (1.0 / acc[HEAD_DIM:HEAD_DIM + 1]))
        o_ref[0, :, g * LANES:(g + 1) * LANES] = jnp.concatenate(pair, axis=0).T.astype(BF16)


def _fox_attn(qx, kx, vt):
    shape3 = (BATCH, LP, D_X)
    n_heads = 2 * ATTN_PAIRS
    q_spec = pl.BlockSpec((1, TM, ATTN_PAIRS * PAIR_W), lambda b, p, i: (b, i, p))
    k_spec = pl.BlockSpec((1, LP, ATTN_PAIRS * PAIR_W), lambda b, p, i: (b, 0, p))
    vt_spec = pl.BlockSpec((None, n_heads, NT, VT_ROWS, TM), lambda b, p, i: (b, p, 0, 0, 0))
    return pl.pallas_call(
        _fox_attn_body,
        name="fox_attn",
        grid=(BATCH, N_PAIRS // ATTN_PAIRS, NT),
        in_specs=[q_spec, k_spec, vt_spec],
        out_specs=pl.BlockSpec((1, TM, ATTN_PAIRS * LANES), lambda b, p, i: (b, i, p)),
        out_shape=jax.ShapeDtypeStruct((BATCH, LP, D_MODEL), BF16),
        scratch_shapes=[pltpu.VMEM((n_heads, 2 * TM, TM), F32),
                        pltpu.VMEM((n_heads, 1, TM), F32),
                        pltpu.VMEM((n_heads, VT_ROWS, TM), F32)],
        compiler_params=_params(("arbitrary", "arbitrary", "arbitrary")),
    )(qx.reshape(shape3), kx.reshape(shape3), vt).reshape(T, D_MODEL)


def _bias_selector(first_lane):
    r = jnp.arange(N_SPLIT * LANES)
    piece, head = r // LANES, r % LANES
    col = (head // 2) * LANES + (head % 2) * AUG_STRIDE + first_lane + piece
    hit = (jnp.arange(N_PAIRS * LANES)[None, :] == col[:, None]) & (head < N_HEADS)[:, None]
    return hit.astype(BF16)


def _fox(h, g, w_in, b_f, q_g, k_g, w_out):
    wq = w_in[:, :D_MODEL].astype(BF16)
    wk = w_in[:, D_MODEL:2 * D_MODEL].astype(BF16)
    wv = w_in[:, 2 * D_MODEL:3 * D_MODEL].astype(BF16)
    wf = jnp.pad(w_in[:, 3 * D_MODEL:], ((0, 0), (0, LANES - N_HEADS))).astype(BF16)
    bf = jnp.pad(b_f, (0, LANES - N_HEADS)).reshape(1, LANES)
    gq = jnp.tile(q_g, 2).reshape(1, LANES)
    gk = jnp.tile(k_g, 2).reshape(1, LANES)
    tri = (jnp.arange(TM)[None, :] <= jnp.arange(TM)[:, None]).astype(BF16)
    qx, kx, vt = _fox_proj(h, g, wq, wk, wv, wf, bf, gq, gk, tri,
                           _bias_selector(0), _bias_selector(N_SPLIT))
    return _fox_attn(qx, kx, vt), w_out.astype(BF16)


def kernel(x, meta, ffn_norm, ffn_w_gate, ffn_w_up, ffn_w_down, mix_norm, a_w_in, a_conv, a_w_out, b_w_in, b_conv, b_conv_bias, b_ln_g, b_ln_b, b_w_out, c_w_in, c_b_f, c_q_norm, c_k_norm, c_w_out):
    front = jnp.concatenate([jnp.zeros((PAD, D_MODEL), x.dtype), meta.astype(x.dtype)], axis=0)
    ffn_g = ffn_norm.reshape(DEPTH, 2, 1, D_MODEL)
    ffn_f32 = (ffn_w_gate, ffn_w_up, ffn_w_down)
    w = tuple(a[0, 0].astype(BF16) for a in ffn_f32)
    mixa_params = (mix_norm.reshape(DEPTH, 1, D_MODEL), a_w_in.astype(BF16), a_conv,
                   a_w_out.astype(BF16))
    h = None
    for i in range(DEPTH):
        m, j = i % N_MIXERS, i // N_MIXERS
        if i == 0:
            h, w = _ffn_first(x, front, (i, 0), ffn_g, w, (i, 1), ffn_f32)
        else:
            h, w = _ffn(h, (i, 0), ffn_g, w, (i, 1), ffn_f32)
        if m == 0:
            h = _mixa(h, i, j, *mixa_params)
        elif m == 1:
            h = _mixb(h, mix_norm[i], b_w_in[j].astype(BF16), b_conv[j], b_conv_bias[j],
                      b_ln_g[j], b_ln_b[j], b_w_out[j].astype(BF16))
        else:
            assert i < DEPTH - 1
            o, wo = _fox(h, mix_norm[i], c_w_in[j], c_b_f[j], c_q_norm[j], c_k_norm[j], c_w_out[j])
            h, w = _ffn_after_attn(h, o, wo, (i, 1), ffn_g, w, (i + 1, 0), ffn_f32)
            continue
        if i == DEPTH - 1:
            h = _ffn_last(h, (i, 1), ffn_g, w)
        else:
            h, w = _ffn(h, (i, 1), ffn_g, w, (i + 1, 0), ffn_f32)
    return h
```

```python
import jax
import jax.numpy as jnp
from jax import lax
from jax.experimental import pallas as pl
from jax.experimental.pallas import tpu as pltpu

D_MODEL = 1024
BATCH = 2
SEQ = 8192
DEPTH = 4
N_MIXERS = 3
N_META = 16
D_FF = 2816
SC_WIDTH = 3
CONF_WIDTH = 31
N_HEADS = 16
HEAD_DIM = D_MODEL // N_HEADS
EPS = 1e-6
MASK_VALUE = -1e30
LOG2E = 1.4426950408889634

LANES = 128
SUBLANES = 8
PAD = (-(N_META + SEQ)) % LANES
LP = PAD + N_META + SEQ
T = BATCH * LP
FRONT = PAD + N_META
TM = 640
NT = LP // TM
TM_A = 1040
TM_FFN = 832
TM_LAST = 1024
CAST_STEPS = 16
N_PAIRS = N_HEADS // 2
PAIR_W = 2 * LANES
D_X = N_PAIRS * PAIR_W
N_SPLIT = 3
AUG_STRIDE = 8
ATTN_PAIRS = 2
VT_ROWS = HEAD_DIM + 16
SC_HALO = 8
CONF_HALO = 32
CONF_ROWS = 64
VMEM_LIMIT = 56 * 1024 * 1024

F32 = jnp.float32
BF16 = jnp.bfloat16


def _rms(x, g):
    return x * lax.rsqrt(jnp.mean(x * x, axis=-1, keepdims=True) + EPS) * g


def _sigmoid(x):
    return 1.0 / (1.0 + jnp.exp(-x))


def _dot(a, b):
    return jnp.dot(a, b, preferred_element_type=F32)


def _split3(x):
    hi = x.astype(BF16)
    r1 = x - hi.astype(F32)
    mid = r1.astype(BF16)
    lo = (r1 - mid.astype(F32)).astype(BF16)
    return hi, mid, lo


def _row_in_batch(tile_in_batch, tile=TM):
    return tile_in_batch * tile + lax.broadcasted_iota(jnp.int32, (tile, 1), 0)


def _const_spec(shape):
    return pl.BlockSpec(shape, lambda *_: (0,) * len(shape), pipeline_mode=pl.Buffered(1))


def _picked_spec(pick, tail):
    return pl.BlockSpec((None,) * len(pick) + tail, lambda *_: tuple(pick) + (0,) * len(tail),
                        pipeline_mode=pl.Buffered(1))


def _row_spec(width):
    return pl.BlockSpec((TM, width), lambda i: (i, 0))


def _params(sem):
    return pltpu.CompilerParams(dimension_semantics=sem, vmem_limit_bytes=VMEM_LIMIT)


def _ffn_rows(h, g_ref, wg_ref, wu_ref, wd_ref):
    x = _rms(h, g_ref[...]).astype(BF16)
    gate = _dot(x, wg_ref[...])
    up = _dot(x, wu_ref[...])
    a = (gate * _sigmoid(gate) * up).astype(BF16)
    return h + 0.5 * _dot(a, wd_ref[...])


def _cast_next(refs):
    for src, dst in zip(refs[:3], refs[3:]):
        dst[...] = src[...].astype(BF16)


def _ffn_body(h_ref, g_ref, wg_ref, wu_ref, wd_ref, *rest):
    o_ref = rest[-1] if len(rest) == 1 else rest[3]
    o_ref[...] = _ffn_rows(h_ref[...], g_ref, wg_ref, wu_ref, wd_ref)
    if len(rest) > 1:
        _cast_next(rest[:3] + rest[4:])


def _ffn_after_attn_body(h_ref, o_ref, wo_ref, g_ref, wg_ref, wu_ref, wd_ref, *rest):
    h = h_ref[...] + _dot(o_ref[...], wo_ref[...])
    rest[3][...] = _ffn_rows(h, g_ref, wg_ref, wu_ref, wd_ref)
    _cast_next(rest[:3] + rest[4:])


def _ffn_first_body(x_ref, front_ref, g_ref, wg_ref, wu_ref, wd_ref, nwg_ref, nwu_ref, nwd_ref,
                    o_ref, cwg_ref, cwu_ref, cwd_ref):
    xb = x_ref[...]
    head = jnp.concatenate([front_ref[...], xb[:TM_FFN - FRONT]], axis=0)
    h = jnp.where(pl.program_id(1) == 0, head, xb)
    o_ref[...] = _ffn_rows(h, g_ref, wg_ref, wu_ref, wd_ref)
    _cast_next((nwg_ref, nwu_ref, nwd_ref, cwg_ref, cwu_ref, cwd_ref))


def _ffn_weight_specs(pick):
    return [_picked_spec(pick, (1, D_MODEL)), _const_spec((D_MODEL, D_FF)),
            _const_spec((D_MODEL, D_FF)), _const_spec((D_FF, D_MODEL))]


def _cast_specs(pick, step_of):
    def chunk(*ids):
        return jnp.minimum(step_of(*ids), CAST_STEPS - 1)

    shapes = [(D_MODEL, D_FF), (D_MODEL, D_FF), (D_FF, D_MODEL)]
    in_specs = [pl.BlockSpec((None, None, r // CAST_STEPS, c), lambda *ids: (*pick, chunk(*ids), 0))
                for r, c in shapes]
    out_specs = [pl.BlockSpec((r // CAST_STEPS, c), lambda *ids: (chunk(*ids), 0)) for r, c in shapes]
    out_shapes = [jax.ShapeDtypeStruct(sh, BF16) for sh in shapes]
    return in_specs, out_specs, out_shapes


def _ffn(h, pick, g, w, nxt, w_f32):
    c_in, c_out, c_shapes = _cast_specs(nxt, lambda i: i)
    out = pl.pallas_call(
        _ffn_body,
        name="ffn",
        grid=(T // TM_FFN,),
        in_specs=[pl.BlockSpec((TM_FFN, D_MODEL), lambda i: (i, 0))] + _ffn_weight_specs(pick) + c_in,
        out_specs=[pl.BlockSpec((TM_FFN, D_MODEL), lambda i: (i, 0))] + c_out,
        out_shape=[jax.ShapeDtypeStruct((T, D_MODEL), F32)] + c_shapes,
        compiler_params=_params(("arbitrary",)),
    )(h, g, *w, *w_f32)
    return out[0], tuple(out[1:])


def _ffn_after_attn(h, o, w_out, pick, g, w, nxt, w_f32):
    tile = pl.BlockSpec((TM_FFN, D_MODEL), lambda i: (i, 0))
    c_in, c_out, c_shapes = _cast_specs(nxt, lambda i: i)
    out = pl.pallas_call(
        _ffn_after_attn_body,
        name="ffn_after_attn",
        grid=(T // TM_FFN,),
        in_specs=[tile, tile, _const_spec((D_MODEL, D_MODEL))] + _ffn_weight_specs(pick) + c_in,
        out_specs=[tile] + c_out,
        out_shape=[jax.ShapeDtypeStruct((T, D_MODEL), F32)] + c_shapes,
        compiler_params=_params(("arbitrary",)),
    )(h, o, w_out, g, *w, *w_f32)
    return out[0], tuple(out[1:])


def _ffn_first(x, front, pick, g, w, nxt, w_f32):
    x_spec = pl.BlockSpec(
        (pl.Element(TM_FFN), pl.Element(D_MODEL)),
        lambda b, t: (pl.multiple_of(b * SEQ + jnp.maximum(t * TM_FFN - FRONT, 0), SUBLANES), 0))
    n_tiles = LP // TM_FFN
    c_in, c_out, c_shapes = _cast_specs(nxt, lambda b, t: b * n_tiles + t)
    out = pl.pallas_call(
        _ffn_first_body,
        name="ffn_first",
        grid=(BATCH, n_tiles),
        in_specs=[x_spec, _const_spec((FRONT, D_MODEL))] + _ffn_weight_specs(pick) + c_in,
        out_specs=[pl.BlockSpec((TM_FFN, D_MODEL), lambda b, t: (b * n_tiles + t, 0))] + c_out,
        out_shape=[jax.ShapeDtypeStruct((T, D_MODEL), F32)] + c_shapes,
        compiler_params=_params(("arbitrary", "arbitrary")),
    )(x.reshape(BATCH * SEQ, D_MODEL), front, g, *w, *w_f32)
    return out[0], tuple(out[1:])


def _ffn_last(h, pick, g, w):
    n_tiles = SEQ // TM_LAST
    h_spec = pl.BlockSpec(
        (pl.Element(TM_LAST), pl.Element(D_MODEL)),
        lambda b, t: (pl.multiple_of(b * LP + FRONT + t * TM_LAST, LANES), 0))
    return pl.pallas_call(
        _ffn_body,
        name="ffn_last",
        grid=(BATCH, n_tiles),
        in_specs=[h_spec] + _ffn_weight_specs(pick),
        out_specs=pl.BlockSpec((TM_LAST, D_MODEL), lambda b, t: (b * n_tiles + t, 0)),
        out_shape=jax.ShapeDtypeStruct((BATCH * SEQ, D_MODEL), F32),
        compiler_params=_params(("arbitrary", "arbitrary")),
    )(h, g, *w).reshape(BATCH, SEQ, D_MODEL)


def _mixa_body(h_ref, g_ref, win_ref, cw_ref, wout_ref, o_ref, buf_ref):
    t = pl.program_id(0) % (LP // TM_A)

    @pl.when(t == 0)
    def _():
        buf_ref[0:SC_HALO, :] = jnp.zeros((SC_HALO, D_MODEL), F32)

    h = h_ref[...]
    u = _rms(h, g_ref[...]).astype(BF16)
    p = _dot(u, win_ref[...])
    b_gate = p[:, :D_MODEL]
    cv = p[:, D_MODEL:2 * D_MODEL] * p[:, 2 * D_MODEL:]
    cv = jnp.where(_row_in_batch(t, TM_A) >= PAD, cv, 0.0)
    buf_ref[SC_HALO:SC_HALO + TM_A, :] = cv
    y = cw_ref[SC_WIDTH - 1:SC_WIDTH, :] * cv
    for k in range(SC_WIDTH - 1):
        shift = SC_WIDTH - 1 - k
        y = y + cw_ref[k:k + 1, :] * buf_ref[SC_HALO - shift:SC_HALO - shift + TM_A, :]
    z = (b_gate * y).astype(BF16)
    o_ref[...] = h + _dot(z, wout_ref[...])
    buf_ref[0:SC_HALO, :] = buf_ref[TM_A:TM_A + SC_HALO, :]


def _mixa(h, layer, j, g, w_in, conv_w, w_out):
    tile = pl.BlockSpec((TM_A, D_MODEL), lambda i: (i, 0))
    return pl.pallas_call(
        _mixa_body,
        name="short_conv_mixer",
        grid=(T // TM_A,),
        in_specs=[tile, _picked_spec((layer,), (1, D_MODEL)),
                  _picked_spec((j,), (D_MODEL, 3 * D_MODEL)), _picked_spec((j,), (SC_WIDTH, D_MODEL)),
                  _picked_spec((j,), (D_MODEL, D_MODEL))],
        out_specs=tile,
        out_shape=jax.ShapeDtypeStruct((T, D_MODEL), F32),
        scratch_shapes=[pltpu.VMEM((TM_A + SC_HALO, D_MODEL), F32)],
        compiler_params=_params(("arbitrary",)),
    )(h, g, w_in, conv_w, w_out)


def _mixb_body(h_ref, g_ref, win_ref, cw_ref, cb_ref, lng_ref, lnb_ref, wout_ref,
               o_ref, buf_ref, y_ref):
    t = pl.program_id(0) % NT

    @pl.when(t == 0)
    def _():
        buf_ref[:, 0:CONF_HALO, :] = jnp.zeros((D_MODEL // LANES, CONF_HALO, LANES), F32)

    h = h_ref[...]
    u = _rms(h, g_ref[...]).astype(BF16)
    real = _row_in_batch(t) >= PAD
    n_col = D_MODEL // LANES

    first = CONF_HALO - (CONF_WIDTH - 1)
    for c in range(n_col):
        p = _dot(u, win_ref[:, 2 * c * LANES:2 * (c + 1) * LANES])
        glu = jnp.where(real, p[:, :LANES] * _sigmoid(p[:, LANES:]), 0.0)
        buf_ref[c, CONF_HALO:CONF_HALO + TM, :] = glu
        w = cw_ref[c]
        bias = jnp.broadcast_to(cb_ref[c], (CONF_ROWS, LANES))
        for r in range(0, TM, CONF_ROWS):
            acc = bias
            for k in range(CONF_WIDTH):
                acc = acc + w[k:k + 1, :] * buf_ref[c, r + first + k:r + first + k + CONF_ROWS, :]
            y_ref[c, r:r + CONF_ROWS, :] = acc
        buf_ref[c, 0:CONF_HALO, :] = buf_ref[c, TM:TM + CONF_HALO, :]

    y = jnp.concatenate([y_ref[c] for c in range(n_col)], axis=1)
    mu = jnp.mean(y, axis=-1, keepdims=True)
    yc = y - mu
    var = jnp.mean(yc * yc, axis=-1, keepdims=True)
    ln = yc * lax.rsqrt(var + EPS) * lng_ref[...] + lnb_ref[...]
    s = (ln * _sigmoid(ln)).astype(BF16)
    o_ref[...] = h + _dot(s, wout_ref[...])


def _mixb(h, g, w_in, conv_w, conv_b, ln_g, ln_b, w_out):
    row = lambda v: v.reshape(1, D_MODEL)
    n_col = D_MODEL // LANES
    conv_w = conv_w.reshape(CONF_WIDTH, n_col, LANES).transpose(1, 0, 2)
    conv_b = conv_b.reshape(n_col, 1, LANES)
    w_in = w_in.reshape(D_MODEL, 2, n_col, LANES).transpose(0, 2, 1, 3).reshape(D_MODEL, 2 * D_MODEL)
    return pl.pallas_call(
        _mixb_body,
        name="conformer_mixer",
        grid=(T // TM,),
        in_specs=[_row_spec(D_MODEL), _const_spec((1, D_MODEL)),
                  _const_spec((D_MODEL, 2 * D_MODEL)), _const_spec((n_col, CONF_WIDTH, LANES)),
                  _const_spec((n_col, 1, LANES)), _const_spec((1, D_MODEL)), _const_spec((1, D_MODEL)),
                  _const_spec((D_MODEL, D_MODEL))],
        out_specs=_row_spec(D_MODEL),
        out_shape=jax.ShapeDtypeStruct((T, D_MODEL), F32),
        scratch_shapes=[pltpu.VMEM((n_col, TM + CONF_HALO, LANES), F32),
                        pltpu.VMEM((n_col, TM, LANES), F32)],
        compiler_params=_params(("arbitrary",)),
    )(h, row(g), w_in, conv_w, conv_b, row(ln_g), row(ln_b), w_out)


def _head_lanes(lane, e):
    chan = (lane >= e * HEAD_DIM) & (lane < (e + 1) * HEAD_DIM)
    a = LANES + AUG_STRIDE * e
    return chan | ((lane >= a) & (lane < a + 2 * N_SPLIT))


def _fox_proj_body(h_ref, g_ref, wq_ref, wk_ref, wv_ref, wf_ref, bf_ref, gq_ref, gk_ref,
                   tri_ref, selq_ref, selk_ref, qx_ref, kx_ref, vt_ref, carry_ref):
    t = pl.program_id(0) % NT

    @pl.when(t == 0)
    def _():
        carry_ref[...] = jnp.zeros_like(carry_ref)

    u = _rms(h_ref[...], g_ref[...]).astype(BF16)
    real = _row_in_batch(t) >= PAD

    f_logit = _dot(u, wf_ref[...]) + bf_ref[...]
    log_f = jnp.minimum(f_logit, 0.0) - jnp.log1p(jnp.exp(-jnp.abs(f_logit)))
    log_f = jnp.where(real, log_f, 0.0)
    c3 = _dot(tri_ref[...], jnp.concatenate(_split3(log_f), axis=1))
    cum = (c3[:, :LANES] + c3[:, LANES:2 * LANES] + c3[:, 2 * LANES:]) + carry_ref[0:1, :]
    carry_ref[...] = jnp.broadcast_to(cum[TM - 1:TM, :], carry_ref.shape)

    lane = lax.broadcasted_iota(jnp.int32, (1, LANES), 1)
    sub = lane % AUG_STRIDE
    in_aug = lane < 2 * AUG_STRIDE
    ones_q = (in_aug & (sub >= N_SPLIT) & (sub < 2 * N_SPLIT)).astype(F32)
    ones_k = (in_aug & (sub < N_SPLIT)).astype(F32)
    pad_rows = VT_ROWS - HEAD_DIM
    ones_row = (lax.broadcasted_iota(jnp.int32, (pad_rows, TM), 0) == 0).astype(BF16)

    neg_ck = jnp.where(real, -LOG2E * cum, MASK_VALUE)
    aug_q = _dot(jnp.concatenate(_split3(LOG2E * cum), axis=1), selq_ref[...])
    aug_k = _dot(jnp.concatenate(_split3(neg_ck), axis=1), selk_ref[...])

    first_head = lane < HEAD_DIM
    gq = gq_ref[...] * (LOG2E * HEAD_DIM ** -0.5)
    gk = gk_ref[...]

    def normed(x, gain):
        x2 = x * x
        ms0 = jnp.sum(jnp.where(first_head, x2, 0.0), axis=-1, keepdims=True)
        ms1 = jnp.sum(jnp.where(first_head, 0.0, x2), axis=-1, keepdims=True)
        ms = jnp.where(first_head, ms0, ms1) * (1.0 / HEAD_DIM)
        return x * lax.rsqrt(ms + EPS) * gain

    q = _dot(u, wq_ref[...])
    k = _dot(u, wk_ref[...])
    v = _dot(u, wv_ref[...])
    for j in range(N_PAIRS):
        src = slice(j * LANES, (j + 1) * LANES)
        chan = slice(j * PAIR_W, j * PAIR_W + LANES)
        bias = slice(j * PAIR_W + LANES, (j + 1) * PAIR_W)
        qx_ref[:, chan] = normed(q[:, src], gq).astype(BF16)
        qx_ref[:, bias] = (aug_q[:, src] + ones_q).astype(BF16)
        kx_ref[:, chan] = normed(k[:, src], gk).astype(BF16)
        kx_ref[:, bias] = (aug_k[:, src] + ones_k).astype(BF16)
        v_t = v[:, src].T
        for e in range(2):
            vt_ref[2 * j + e, :HEAD_DIM, :] = v_t[e * HEAD_DIM:(e + 1) * HEAD_DIM].astype(BF16)
            vt_ref[2 * j + e, HEAD_DIM:, :] = ones_row


def _fox_proj(h, g, wq, wk, wv, wf, bf, gq, gk, tri, selq, selk):
    out = jax.ShapeDtypeStruct((T, D_X), BF16)
    w_spec = _const_spec((D_MODEL, D_MODEL))
    sel_spec = _const_spec((N_SPLIT * LANES, D_MODEL))
    return pl.pallas_call(
        _fox_proj_body,
        name="fox_proj",
        grid=(T // TM,),
        in_specs=[_row_spec(D_MODEL), _const_spec((1, D_MODEL)), w_spec, w_spec, w_spec,
                  _const_spec((D_MODEL, LANES)), _const_spec((1, LANES)),
                  _const_spec((1, LANES)), _const_spec((1, LANES)),
                  _const_spec((TM, TM)), sel_spec, sel_spec],
        out_specs=[_row_spec(D_X), _row_spec(D_X),
                   pl.BlockSpec((None, N_HEADS, None, VT_ROWS, TM), lambda i: (i // NT, 0, i % NT, 0, 0))],
        out_shape=[out, out, jax.ShapeDtypeStruct((BATCH, N_HEADS, NT, VT_ROWS, TM), BF16)],
        scratch_shapes=[pltpu.VMEM((8, LANES), F32)],
        compiler_params=_params(("arbitrary",)),
    )(h, g.reshape(1, D_MODEL), wq, wk, wv, wf, bf, gq, gk, tri, selq, selk)


def _fox_attn_body(q_ref, k_ref, vt_ref, o_ref, s_ref, m_ref, acc_ref):
    i = pl.program_id(2)
    lane_x = lax.broadcasted_iota(jnp.int32, (1, PAIR_W), 1)
    heads = [(g, e) for g in range(ATTN_PAIRS) for e in range(2)]
    q_heads = []
    for g, e in heads:
        qx = q_ref[0, :, g * PAIR_W:(g + 1) * PAIR_W]
        q_heads.append(jnp.where(_head_lanes(lane_x, e), qx, jnp.zeros_like(qx)))

    def span(kb, n, ends_on_diagonal=False):
        keys = pl.ds(pl.multiple_of(kb * TM, TM), n * TM)
        for hd, (g, e) in enumerate(heads):
            k = k_ref[0, keys, g * PAIR_W:(g + 1) * PAIR_W]
            s = lax.dot_general(k, q_heads[hd], (((1,), (1,)), ((), ())), preferred_element_type=F32)
            if ends_on_diagonal:
                key = lax.broadcasted_iota(jnp.int32, (n * TM, TM), 0)
                qry = lax.broadcasted_iota(jnp.int32, (n * TM, TM), 1)
                s = jnp.where(key <= qry + (n - 1) * TM, s, MASK_VALUE)
            s_ref[hd, :n * TM, :] = s
        for hd in range(len(heads)):
            s = s_ref[hd, :n * TM, :]
            m_old = m_ref[hd]
            m_new = jnp.maximum(m_old, jnp.max(s, axis=0, keepdims=True))
            p = jnp.exp2(s - m_new).astype(BF16)
            m_ref[hd] = m_new
            v_t = jnp.concatenate([vt_ref[hd, kb + b] for b in range(n)], axis=1)
            acc_ref[hd] = jnp.exp2(m_old - m_new) * acc_ref[hd] + _dot(v_t, p)

    m_ref[...] = jnp.full(m_ref.shape, MASK_VALUE, F32)
    acc_ref[...] = jnp.zeros(acc_ref.shape, F32)

    @pl.when(i % 2 == 0)
    def _():
        span(i, 1, ends_on_diagonal=True)

    @pl.when(i % 2 == 1)
    def _():
        span(i - 1, 2, ends_on_diagonal=True)

    def step(u, carry):
        span(2 * u, 2)
        return carry

    lax.fori_loop(0, i // 2, step, 0)
    for g in range(ATTN_PAIRS):
        pair = []
        for e in range(2):
            acc = acc_ref[2 * g + e]
            pair.append(acc[:HEAD_DIM] * (1.0 / acc[HEAD_DIM:HEAD_DIM + 1]))
        o_ref[0, :, g * LANES:(g + 1) * LANES] = jnp.concatenate(pair, axis=0).T.astype(BF16)


def _fox_attn(qx, kx, vt):
    shape3 = (BATCH, LP, D_X)
    n_heads = 2 * ATTN_PAIRS
    q_spec = pl.BlockSpec((1, TM, ATTN_PAIRS * PAIR_W), lambda b, p, i: (b, i, p))
    k_spec = pl.BlockSpec((1, LP, ATTN_PAIRS * PAIR_W), lambda b, p, i: (b, 0, p))
    vt_spec = pl.BlockSpec((None, n_heads, NT, VT_ROWS, TM), lambda b, p, i: (b, p, 0, 0, 0))
    return pl.pallas_call(
        _fox_attn_body,
        name="fox_attn",
        grid=(BATCH, N_PAIRS // ATTN_PAIRS, NT),
        in_specs=[q_spec, k_spec, vt_spec],
        out_specs=pl.BlockSpec((1, TM, ATTN_PAIRS * LANES), lambda b, p, i: (b, i, p)),
        out_shape=jax.ShapeDtypeStruct((BATCH, LP, D_MODEL), BF16),
        scratch_shapes=[pltpu.VMEM((n_heads, 2 * TM, TM), F32),
                        pltpu.VMEM((n_heads, 1, TM), F32),
                        pltpu.VMEM((n_heads, VT_ROWS, TM), F32)],
        compiler_params=_params(("arbitrary", "arbitrary", "arbitrary")),
    )(qx.reshape(shape3), kx.reshape(shape3), vt).reshape(T, D_MODEL)


def _bias_selector(first_lane):
    r = jnp.arange(N_SPLIT * LANES)
    piece, head = r // LANES, r % LANES
    col = (head // 2) * LANES + (head % 2) * AUG_STRIDE + first_lane + piece
    hit = (jnp.arange(N_PAIRS * LANES)[None, :] == col[:, None]) & (head < N_HEADS)[:, None]
    return hit.astype(BF16)


def _fox(h, g, w_in, b_f, q_g, k_g, w_out):
    wq = w_in[:, :D_MODEL].astype(BF16)
    wk = w_in[:, D_MODEL:2 * D_MODEL].astype(BF16)
    wv = w_in[:, 2 * D_MODEL:3 * D_MODEL].astype(BF16)
    wf = jnp.pad(w_in[:, 3 * D_MODEL:], ((0, 0), (0, LANES - N_HEADS))).astype(BF16)
    bf = jnp.pad(b_f, (0, LANES - N_HEADS)).reshape(1, LANES)
    gq = jnp.tile(q_g, 2).reshape(1, LANES)
    gk = jnp.tile(k_g, 2).reshape(1, LANES)
    tri = (jnp.arange(TM)[None, :] <= jnp.arange(TM)[:, None]).astype(BF16)
    qx, kx, vt = _fox_proj(h, g, wq, wk, wv, wf, bf, gq, gk, tri,
                           _bias_selector(0), _bias_selector(N_SPLIT))
    return _fox_attn(qx, kx, vt), w_out.astype(BF16)


def kernel(x, meta, ffn_norm, ffn_w_gate, ffn_w_up, ffn_w_down, mix_norm, a_w_in, a_conv, a_w_out, b_w_in, b_conv, b_conv_bias, b_ln_g, b_ln_b, b_w_out, c_w_in, c_b_f, c_q_norm, c_k_norm, c_w_out):
    front = jnp.concatenate([jnp.zeros((PAD, D_MODEL), x.dtype), meta.astype(x.dtype)], axis=0)
    ffn_g = ffn_norm.reshape(DEPTH, 2, 1, D_MODEL)
    ffn_f32 = (ffn_w_gate, ffn_w_up, ffn_w_down)
    w = tuple(a[0, 0].astype(BF16) for a in ffn_f32)
    mixa_params = (mix_norm.reshape(DEPTH, 1, D_MODEL), a_w_in.astype(BF16), a_conv,
                   a_w_out.astype(BF16))
    h = None
    for i in range(DEPTH):
        m, j = i % N_MIXERS, i // N_MIXERS
        if i == 0:
            h, w = _ffn_first(x, front, (i, 0), ffn_g, w, (i, 1), ffn_f32)
        else:
            h, w = _ffn(h, (i, 0), ffn_g, w, (i, 1), ffn_f32)
        if m == 0:
            h = _mixa(h, i, j, *mixa_params)
        elif m == 1:
            h = _mixb(h, mix_norm[i], b_w_in[j].astype(BF16), b_conv[j], b_conv_bias[j],
                      b_ln_g[j], b_ln_b[j], b_w_out[j].astype(BF16))
        else:
            assert i < DEPTH - 1
            o, wo = _fox(h, mix_norm[i], c_w_in[j], c_b_f[j], c_q_norm[j], c_k_norm[j], c_w_out[j])
            h, w = _ffn_after_attn(h, o, wo, (i, 1), ffn_g, w, (i + 1, 0), ffn_f32)
            continue
        if i == DEPTH - 1:
            h = _ffn_last(h, (i, 1), ffn_g, w)
        else:
            h, w = _ffn(h, (i, 1), ffn_g, w, (i + 1, 0), ffn_f32)
    return h
```

```python
import jax
import jax.numpy as jnp
from jax import lax
from jax.experimental import pallas as pl
from jax.experimental.pallas import tpu as pltpu

D_MODEL = 1024
BATCH = 2
SEQ = 8192
DEPTH = 4
N_MIXERS = 3
N_META = 16
D_FF = 2816
SC_WIDTH = 3
CONF_WIDTH = 31
N_HEADS = 16
HEAD_DIM = D_MODEL // N_HEADS
EPS = 1e-6
MASK_VALUE = -1e30
LOG2E = 1.4426950408889634

LANES = 128
SUBLANES = 8
PAD = (-(N_META + SEQ)) % LANES
LP = PAD + N_META + SEQ
T = BATCH * LP
FRONT = PAD + N_META
TM = 640
NT = LP // TM
TM_A = 1040
TM_FFN = 832
TM_MID = 1040
TM_LAST = 1024
CAST_STEPS = 16
N_PAIRS = N_HEADS // 2
PAIR_W = 2 * LANES
D_X = N_PAIRS * PAIR_W
N_SPLIT = 3
AUG_STRIDE = 8
ATTN_PAIRS = 2
VT_ROWS = HEAD_DIM + 16
SC_HALO = 8
CONF_HALO = 32
CONF_ROWS = 64
VMEM_LIMIT = 56 * 1024 * 1024

F32 = jnp.float32
BF16 = jnp.bfloat16


def _rms(x, g):
    return x * lax.rsqrt(jnp.mean(x * x, axis=-1, keepdims=True) + EPS) * g


def _sigmoid(x):
    return 1.0 / (1.0 + jnp.exp(-x))


def _dot(a, b):
    return jnp.dot(a, b, preferred_element_type=F32)


def _split3(x):
    hi = x.astype(BF16)
    r1 = x - hi.astype(F32)
    mid = r1.astype(BF16)
    lo = (r1 - mid.astype(F32)).astype(BF16)
    return hi, mid, lo


def _row_in_batch(tile_in_batch, tile=TM):
    return tile_in_batch * tile + lax.broadcasted_iota(jnp.int32, (tile, 1), 0)


def _const_spec(shape):
    return pl.BlockSpec(shape, lambda *_: (0,) * len(shape), pipeline_mode=pl.Buffered(1))


def _picked_spec(pick, tail):
    return pl.BlockSpec((None,) * len(pick) + tail, lambda *_: tuple(pick) + (0,) * len(tail),
                        pipeline_mode=pl.Buffered(1))


def _row_spec(width):
    return pl.BlockSpec((TM, width), lambda i: (i, 0))


def _params(sem):
    return pltpu.CompilerParams(dimension_semantics=sem, vmem_limit_bytes=VMEM_LIMIT)


def _ffn_rows(h, g_ref, wg_ref, wu_ref, wd_ref):
    x = _rms(h, g_ref[...]).astype(BF16)
    gate = _dot(x, wg_ref[...])
    up = _dot(x, wu_ref[...])
    a = (gate * _sigmoid(gate) * up).astype(BF16)
    return h + 0.5 * _dot(a, wd_ref[...])


def _cast_next(refs):
    for src, dst in zip(refs[:3], refs[3:]):
        dst[...] = src[...].astype(BF16)


def _ffn_body(h_ref, g_ref, wg_ref, wu_ref, wd_ref, *rest):
    o_ref = rest[-1] if len(rest) == 1 else rest[3]
    o_ref[...] = _ffn_rows(h_ref[...], g_ref, wg_ref, wu_ref, wd_ref)
    if len(rest) > 1:
        _cast_next(rest[:3] + rest[4:])


def _ffn_after_attn_body(h_ref, o_ref, wo_ref, g_ref, wg_ref, wu_ref, wd_ref, *rest):
    h = h_ref[...] + _dot(o_ref[...], wo_ref[...])
    rest[3][...] = _ffn_rows(h, g_ref, wg_ref, wu_ref, wd_ref)
    _cast_next(rest[:3] + rest[4:])


def _ffn_first_body(x_ref, front_ref, g_ref, wg_ref, wu_ref, wd_ref, nwg_ref, nwu_ref, nwd_ref,
                    o_ref, cwg_ref, cwu_ref, cwd_ref):
    xb = x_ref[...]
    head = jnp.concatenate([front_ref[...], xb[:TM_FFN - FRONT]], axis=0)
    h = jnp.where(pl.program_id(1) == 0, head, xb)
    o_ref[...] = _ffn_rows(h, g_ref, wg_ref, wu_ref, wd_ref)
    _cast_next((nwg_ref, nwu_ref, nwd_ref, cwg_ref, cwu_ref, cwd_ref))


def _ffn_weight_specs(pick):
    return [_picked_spec(pick, (1, D_MODEL)), _const_spec((D_MODEL, D_FF)),
            _const_spec((D_MODEL, D_FF)), _const_spec((D_FF, D_MODEL))]


def _cast_specs(pick, step_of):
    def chunk(*ids):
        return jnp.minimum(step_of(*ids), CAST_STEPS - 1)

    shapes = [(D_MODEL, D_FF), (D_MODEL, D_FF), (D_FF, D_MODEL)]
    in_specs = [pl.BlockSpec((None, None, r // CAST_STEPS, c), lambda *ids: (*pick, chunk(*ids), 0))
                for r, c in shapes]
    out_specs = [pl.BlockSpec((r // CAST_STEPS, c), lambda *ids: (chunk(*ids), 0)) for r, c in shapes]
    out_shapes = [jax.ShapeDtypeStruct(sh, BF16) for sh in shapes]
    return in_specs, out_specs, out_shapes


def _ffn(h, pick, g, w, nxt, w_f32):
    c_in, c_out, c_shapes = _cast_specs(nxt, lambda i: i)
    out = pl.pallas_call(
        _ffn_body,
        name="ffn",
        grid=(T // TM_MID,),
        in_specs=[pl.BlockSpec((TM_MID, D_MODEL), lambda i: (i, 0))] + _ffn_weight_specs(pick) + c_in,
        out_specs=[pl.BlockSpec((TM_MID, D_MODEL), lambda i: (i, 0))] + c_out,
        out_shape=[jax.ShapeDtypeStruct((T, D_MODEL), F32)] + c_shapes,
        compiler_params=_params(("arbitrary",)),
    )(h, g, *w, *w_f32)
    return out[0], tuple(out[1:])


def _ffn_after_attn(h, o, w_out, pick, g, w, nxt, w_f32):
    tile = pl.BlockSpec((TM_FFN, D_MODEL), lambda i: (i, 0))
    c_in, c_out, c_shapes = _cast_specs(nxt, lambda i: i)
    out = pl.pallas_call(
        _ffn_after_attn_body,
        name="ffn_after_attn",
        grid=(T // TM_FFN,),
        in_specs=[tile, tile, _const_spec((D_MODEL, D_MODEL))] + _ffn_weight_specs(pick) + c_in,
        out_specs=[tile] + c_out,
        out_shape=[jax.ShapeDtypeStruct((T, D_MODEL), F32)] + c_shapes,
        compiler_params=_params(("arbitrary",)),
    )(h, o, w_out, g, *w, *w_f32)
    return out[0], tuple(out[1:])


def _ffn_first(x, front, pick, g, w, nxt, w_f32):
    x_spec = pl.BlockSpec(
        (pl.Element(TM_FFN), pl.Element(D_MODEL)),
        lambda b, t: (pl.multiple_of(b * SEQ + jnp.maximum(t * TM_FFN - FRONT, 0), SUBLANES), 0))
    n_tiles = LP // TM_FFN
    c_in, c_out, c_shapes = _cast_specs(nxt, lambda b, t: b * n_tiles + t)
    out = pl.pallas_call(
        _ffn_first_body,
        name="ffn_first",
        grid=(BATCH, n_tiles),
        in_specs=[x_spec, _const_spec((FRONT, D_MODEL))] + _ffn_weight_specs(pick) + c_in,
        out_specs=[pl.BlockSpec((TM_FFN, D_MODEL), lambda b, t: (b * n_tiles + t, 0))] + c_out,
        out_shape=[jax.ShapeDtypeStruct((T, D_MODEL), F32)] + c_shapes,
        compiler_params=_params(("arbitrary", "arbitrary")),
    )(x.reshape(BATCH * SEQ, D_MODEL), front, g, *w, *w_f32)
    return out[0], tuple(out[1:])


def _ffn_last(h, pick, g, w):
    n_tiles = SEQ // TM_LAST
    h_spec = pl.BlockSpec(
        (pl.Element(TM_LAST), pl.Element(D_MODEL)),
        lambda b, t: (pl.multiple_of(b * LP + FRONT + t * TM_LAST, LANES), 0))
    return pl.pallas_call(
        _ffn_body,
        name="ffn_last",
        grid=(BATCH, n_tiles),
        in_specs=[h_spec] + _ffn_weight_specs(pick),
        out_specs=pl.BlockSpec((TM_LAST, D_MODEL), lambda b, t: (b * n_tiles + t, 0)),
        out_shape=jax.ShapeDtypeStruct((BATCH * SEQ, D_MODEL), F32),
        compiler_params=_params(("arbitrary", "arbitrary")),
    )(h, g, *w).reshape(BATCH, SEQ, D_MODEL)


def _mixa_body(h_ref, g_ref, win_ref, cw_ref, wout_ref, o_ref, buf_ref):
    t = pl.program_id(0) % (LP // TM_A)

    @pl.when(t == 0)
    def _():
        buf_ref[0:SC_HALO, :] = jnp.zeros((SC_HALO, D_MODEL), F32)

    h = h_ref[...]
    u = _rms(h, g_ref[...]).astype(BF16)
    p = _dot(u, win_ref[...])
    b_gate = p[:, :D_MODEL]
    cv = p[:, D_MODEL:2 * D_MODEL] * p[:, 2 * D_MODEL:]
    cv = jnp.where(_row_in_batch(t, TM_A) >= PAD, cv, 0.0)
    buf_ref[SC_HALO:SC_HALO + TM_A, :] = cv
    y = cw_ref[SC_WIDTH - 1:SC_WIDTH, :] * cv
    for k in range(SC_WIDTH - 1):
        shift = SC_WIDTH - 1 - k
        y = y + cw_ref[k:k + 1, :] * buf_ref[SC_HALO - shift:SC_HALO - shift + TM_A, :]
    z = (b_gate * y).astype(BF16)
    o_ref[...] = h + _dot(z, wout_ref[...])
    buf_ref[0:SC_HALO, :] = buf_ref[TM_A:TM_A + SC_HALO, :]


def _mixa(h, layer, j, g, w_in, conv_w, w_out):
    tile = pl.BlockSpec((TM_A, D_MODEL), lambda i: (i, 0))
    return pl.pallas_call(
        _mixa_body,
        name="short_conv_mixer",
        grid=(T // TM_A,),
        in_specs=[tile, _picked_spec((layer,), (1, D_MODEL)),
                  _picked_spec((j,), (D_MODEL, 3 * D_MODEL)), _picked_spec((j,), (SC_WIDTH, D_MODEL)),
                  _picked_spec((j,), (D_MODEL, D_MODEL))],
        out_specs=tile,
        out_shape=jax.ShapeDtypeStruct((T, D_MODEL), F32),
        scratch_shapes=[pltpu.VMEM((TM_A + SC_HALO, D_MODEL), F32)],
        compiler_params=_params(("arbitrary",)),
    )(h, g, w_in, conv_w, w_out)


def _mixb_body(h_ref, g_ref, win_ref, cw_ref, cb_ref, lng_ref, lnb_ref, wout_ref,
               o_ref, buf_ref, y_ref):
    t = pl.program_id(0) % NT

    @pl.when(t == 0)
    def _():
        buf_ref[:, 0:CONF_HALO, :] = jnp.zeros((D_MODEL // LANES, CONF_HALO, LANES), F32)

    h = h_ref[...]
    u = _rms(h, g_ref[...]).astype(BF16)
    real = _row_in_batch(t) >= PAD
    n_col = D_MODEL // LANES

    first = CONF_HALO - (CONF_WIDTH - 1)
    for c in range(n_col):
        p = _dot(u, win_ref[:, 2 * c * LANES:2 * (c + 1) * LANES])
        glu = jnp.where(real, p[:, :LANES] * _sigmoid(p[:, LANES:]), 0.0)
        buf_ref[c, CONF_HALO:CONF_HALO + TM, :] = glu
        w = cw_ref[c]
        bias = jnp.broadcast_to(cb_ref[c], (CONF_ROWS, LANES))
        for r in range(0, TM, CONF_ROWS):
            acc = bias
            for k in range(CONF_WIDTH):
                acc = acc + w[k:k + 1, :] * buf_ref[c, r + first + k:r + first + k + CONF_ROWS, :]
            y_ref[c, r:r + CONF_ROWS, :] = acc
        buf_ref[c, 0:CONF_HALO, :] = buf_ref[c, TM:TM + CONF_HALO, :]

    y = jnp.concatenate([y_ref[c] for c in range(n_col)], axis=1)
    mu = jnp.mean(y, axis=-1, keepdims=True)
    yc = y - mu
    var = jnp.mean(yc * yc, axis=-1, keepdims=True)
    ln = yc * lax.rsqrt(var + EPS) * lng_ref[...] + lnb_ref[...]
    s = (ln * _sigmoid(ln)).astype(BF16)
    o_ref[...] = h + _dot(s, wout_ref[...])


def _mixb(h, g, w_in, conv_w, conv_b, ln_g, ln_b, w_out):
    row = lambda v: v.reshape(1, D_MODEL)
    n_col = D_MODEL // LANES
    conv_w = conv_w.reshape(CONF_WIDTH, n_col, LANES).transpose(1, 0, 2)
    conv_b = conv_b.reshape(n_col, 1, LANES)
    w_in = w_in.reshape(D_MODEL, 2, n_col, LANES).transpose(0, 2, 1, 3).reshape(D_MODEL, 2 * D_MODEL)
    return pl.pallas_call(
        _mixb_body,
        name="conformer_mixer",
        grid=(T // TM,),
        in_specs=[_row_spec(D_MODEL), _const_spec((1, D_MODEL)),
                  _const_spec((D_MODEL, 2 * D_MODEL)), _const_spec((n_col, CONF_WIDTH, LANES)),
                  _const_spec((n_col, 1, LANES)), _const_spec((1, D_MODEL)), _const_spec((1, D_MODEL)),
                  _const_spec((D_MODEL, D_MODEL))],
        out_specs=_row_spec(D_MODEL),
        out_shape=jax.ShapeDtypeStruct((T, D_MODEL), F32),
        scratch_shapes=[pltpu.VMEM((n_col, TM + CONF_HALO, LANES), F32),
                        pltpu.VMEM((n_col, TM, LANES), F32)],
        compiler_params=_params(("arbitrary",)),
    )(h, row(g), w_in, conv_w, conv_b, row(ln_g), row(ln_b), w_out)


def _head_lanes(lane, e):
    chan = (lane >= e * HEAD_DIM) & (lane < (e + 1) * HEAD_DIM)
    a = LANES + AUG_STRIDE * e
    return chan | ((lane >= a) & (lane < a + 2 * N_SPLIT))


def _fox_proj_body(h_ref, g_ref, wq_ref, wk_ref, wv_ref, wf_ref, bf_ref, gq_ref, gk_ref,
                   tri_ref, selq_ref, selk_ref, qx_ref, kx_ref, vt_ref, carry_ref):
    t = pl.program_id(0) % NT

    @pl.when(t == 0)
    def _():
        carry_ref[...] = jnp.zeros_like(carry_ref)

    u = _rms(h_ref[...], g_ref[...]).astype(BF16)
    real = _row_in_batch(t) >= PAD

    f_logit = _dot(u, wf_ref[...]) + bf_ref[...]
    log_f = jnp.minimum(f_logit, 0.0) - jnp.log1p(jnp.exp(-jnp.abs(f_logit)))
    log_f = jnp.where(real, log_f, 0.0)
    c3 = _dot(tri_ref[...], jnp.concatenate(_split3(log_f), axis=1))
    cum = (c3[:, :LANES] + c3[:, LANES:2 * LANES] + c3[:, 2 * LANES:]) + carry_ref[0:1, :]
    carry_ref[...] = jnp.broadcast_to(cum[TM - 1:TM, :], carry_ref.shape)

    lane = lax.broadcasted_iota(jnp.int32, (1, LANES), 1)
    sub = lane % AUG_STRIDE
    in_aug = lane < 2 * AUG_STRIDE
    ones_q = (in_aug & (sub >= N_SPLIT) & (sub < 2 * N_SPLIT)).astype(F32)
    ones_k = (in_aug & (sub < N_SPLIT)).astype(F32)
    pad_rows = VT_ROWS - HEAD_DIM
    ones_row = (lax.broadcasted_iota(jnp.int32, (pad_rows, TM), 0) == 0).astype(BF16)

    neg_ck = jnp.where(real, -LOG2E * cum, MASK_VALUE)
    aug_q = _dot(jnp.concatenate(_split3(LOG2E * cum), axis=1), selq_ref[...])
    aug_k = _dot(jnp.concatenate(_split3(neg_ck), axis=1), selk_ref[...])

    first_head = lane < HEAD_DIM
    gq = gq_ref[...] * (LOG2E * HEAD_DIM ** -0.5)
    gk = gk_ref[...]

    def normed(x, gain):
        x2 = x * x
        ms0 = jnp.sum(jnp.where(first_head, x2, 0.0), axis=-1, keepdims=True)
        ms1 = jnp.sum(jnp.where(first_head, 0.0, x2), axis=-1, keepdims=True)
        ms = jnp.where(first_head, ms0, ms1) * (1.0 / HEAD_DIM)
        return x * lax.rsqrt(ms + EPS) * gain

    q = _dot(u, wq_ref[...])
    k = _dot(u, wk_ref[...])
    v = _dot(u, wv_ref[...])
    for j in range(N_PAIRS):
        src = slice(j * LANES, (j + 1) * LANES)
        chan = slice(j * PAIR_W, j * PAIR_W + LANES)
        bias = slice(j * PAIR_W + LANES, (j + 1) * PAIR_W)
        qx_ref[:, chan] = normed(q[:, src], gq).astype(BF16)
        qx_ref[:, bias] = (aug_q[:, src] + ones_q).astype(BF16)
        kx_ref[:, chan] = normed(k[:, src], gk).astype(BF16)
        kx_ref[:, bias] = (aug_k[:, src] + ones_k).astype(BF16)
        v_t = v[:, src].T
        for e in range(2):
            vt_ref[2 * j + e, :HEAD_DIM, :] = v_t[e * HEAD_DIM:(e + 1) * HEAD_DIM].astype(BF16)
            vt_ref[2 * j + e, HEAD_DIM:, :] = ones_row


def _fox_proj(h, g, wq, wk, wv, wf, bf, gq, gk, tri, selq, selk):
    out = jax.ShapeDtypeStruct((T, D_X), BF16)
    w_spec = _const_spec((D_MODEL, D_MODEL))
    sel_spec = _const_spec((N_SPLIT * LANES, D_MODEL))
    return pl.pallas_call(
        _fox_proj_body,
        name="fox_proj",
        grid=(T // TM,),
        in_specs=[_row_spec(D_MODEL), _const_spec((1, D_MODEL)), w_spec, w_spec, w_spec,
                  _const_spec((D_MODEL, LANES)), _const_spec((1, LANES)),
                  _const_spec((1, LANES)), _const_spec((1, LANES)),
                  _const_spec((TM, TM)), sel_spec, sel_spec],
        out_specs=[_row_spec(D_X), _row_spec(D_X),
                   pl.BlockSpec((None, N_HEADS, None, VT_ROWS, TM), lambda i: (i // NT, 0, i % NT, 0, 0))],
        out_shape=[out, out, jax.ShapeDtypeStruct((BATCH, N_HEADS, NT, VT_ROWS, TM), BF16)],
        scratch_shapes=[pltpu.VMEM((8, LANES), F32)],
        compiler_params=_params(("arbitrary",)),
    )(h, g.reshape(1, D_MODEL), wq, wk, wv, wf, bf, gq, gk, tri, selq, selk)


def _fox_attn_body(q_ref, k_ref, vt_ref, o_ref, s_ref, m_ref, acc_ref):
    i = pl.program_id(2)
    lane_x = lax.broadcasted_iota(jnp.int32, (1, PAIR_W), 1)
    heads = [(g, e) for g in range(ATTN_PAIRS) for e in range(2)]
    q_heads = []
    for g, e in heads:
        qx = q_ref[0, :, g * PAIR_W:(g + 1) * PAIR_W]
        q_heads.append(jnp.where(_head_lanes(lane_x, e), qx, jnp.zeros_like(qx)))

    def span(kb, n, ends_on_diagonal=False):
        keys = pl.ds(pl.multiple_of(kb * TM, TM), n * TM)
        for hd, (g, e) in enumerate(heads):
            k = k_ref[0, keys, g * PAIR_W:(g + 1) * PAIR_W]
            s = lax.dot_general(k, q_heads[hd], (((1,), (1,)), ((), ())), preferred_element_type=F32)
            if ends_on_diagonal:
                key = lax.broadcasted_iota(jnp.int32, (n * TM, TM), 0)
                qry = lax.broadcasted_iota(jnp.int32, (n * TM, TM), 1)
                s = jnp.where(key <= qry + (n - 1) * TM, s, MASK_VALUE)
            s_ref[hd, :n * TM, :] = s
        for hd in range(len(heads)):
            s = s_ref[hd, :n * TM, :]
            m_old = m_ref[hd]
            m_new = jnp.maximum(m_old, jnp.max(s, axis=0, keepdims=True))
            p = jnp.exp2(s - m_new).astype(BF16)
            m_ref[hd] = m_new
            v_t = jnp.concatenate([vt_ref[hd, kb + b] for b in range(n)], axis=1)
            acc_ref[hd] = jnp.exp2(m_old - m_new) * acc_ref[hd] + _dot(v_t, p)

    m_ref[...] = jnp.full(m_ref.shape, MASK_VALUE, F32)
    acc_ref[...] = jnp.zeros(acc_ref.shape, F32)

    @pl.when(i % 2 == 0)
    def _():
        span(i, 1, ends_on_diagonal=True)

    @pl.when(i % 2 == 1)
    def _():
        span(i - 1, 2, ends_on_diagonal=True)

    def step(u, carry):
        span(2 * u, 2)
        return carry

    lax.fori_loop(0, i // 2, step, 0)
    for g in range(ATTN_PAIRS):
        pair = []
        for e in range(2):
            acc = acc_ref[2 * g + e]
            pair.append(acc[:HEAD_DIM] * (1.0 / acc[HEAD_DIM:HEAD_DIM + 1]))
        o_ref[0, :, g * LANES:(g + 1) * LANES] = jnp.concatenate(pair, axis=0).T.astype(BF16)


def _fox_attn(qx, kx, vt):
    shape3 = (BATCH, LP, D_X)
    n_heads = 2 * ATTN_PAIRS
    q_spec = pl.BlockSpec((1, TM, ATTN_PAIRS * PAIR_W), lambda b, p, i: (b, i, p))
    k_spec = pl.BlockSpec((1, LP, ATTN_PAIRS * PAIR_W), lambda b, p, i: (b, 0, p))
    vt_spec = pl.BlockSpec((None, n_heads, NT, VT_ROWS, TM), lambda b, p, i: (b, p, 0, 0, 0))
    return pl.pallas_call(
        _fox_attn_body,
        name="fox_attn",
        grid=(BATCH, N_PAIRS // ATTN_PAIRS, NT),
        in_specs=[q_spec, k_spec, vt_spec],
        out_specs=pl.BlockSpec((1, TM, ATTN_PAIRS * LANES), lambda b, p, i: (b, i, p)),
        out_shape=jax.ShapeDtypeStruct((BATCH, LP, D_MODEL), BF16),
        scratch_shapes=[pltpu.VMEM((n_heads, 2 * TM, TM), F32),
                        pltpu.VMEM((n_heads, 1, TM), F32),
                        pltpu.VMEM((n_heads, VT_ROWS, TM), F32)],
        compiler_params=_params(("arbitrary", "arbitrary", "arbitrary")),
    )(qx.reshape(shape3), kx.reshape(shape3), vt).reshape(T, D_MODEL)


def _bias_selector(first_lane):
    r = jnp.arange(N_SPLIT * LANES)
    piece, head = r // LANES, r % LANES
    col = (head // 2) * LANES + (head % 2) * AUG_STRIDE + first_lane + piece
    hit = (jnp.arange(N_PAIRS * LANES)[None, :] == col[:, None]) & (head < N_HEADS)[:, None]
    return hit.astype(BF16)


def _fox(h, g, w_in, b_f, q_g, k_g, w_out):
    wq = w_in[:, :D_MODEL].astype(BF16)
    wk = w_in[:, D_MODEL:2 * D_MODEL].astype(BF16)
    wv = w_in[:, 2 * D_MODEL:3 * D_MODEL].astype(BF16)
    wf = jnp.pad(w_in[:, 3 * D_MODEL:], ((0, 0), (0, LANES - N_HEADS))).astype(BF16)
    bf = jnp.pad(b_f, (0, LANES - N_HEADS)).reshape(1, LANES)
    gq = jnp.tile(q_g, 2).reshape(1, LANES)
    gk = jnp.tile(k_g, 2).reshape(1, LANES)
    tri = (jnp.arange(TM)[None, :] <= jnp.arange(TM)[:, None]).astype(BF16)
    qx, kx, vt = _fox_proj(h, g, wq, wk, wv, wf, bf, gq, gk, tri,
                           _bias_selector(0), _bias_selector(N_SPLIT))
    return _fox_attn(qx, kx, vt), w_out.astype(BF16)


def kernel(x, meta, ffn_norm, ffn_w_gate, ffn_w_up, ffn_w_down, mix_norm, a_w_in, a_conv, a_w_out, b_w_in, b_conv, b_conv_bias, b_ln_g, b_ln_b, b_w_out, c_w_in, c_b_f, c_q_norm, c_k_norm, c_w_out):
    front = jnp.concatenate([jnp.zeros((PAD, D_MODEL), x.dtype), meta.astype(x.dtype)], axis=0)
    ffn_g = ffn_norm.reshape(DEPTH, 2, 1, D_MODEL)
    ffn_f32 = (ffn_w_gate, ffn_w_up, ffn_w_down)
    w = tuple(a[0, 0].astype(BF16) for a in ffn_f32)
    mixa_params = (mix_norm.reshape(DEPTH, 1, D_MODEL), a_w_in.astype(BF16), a_conv,
                   a_w_out.astype(BF16))
    h = None
    for i in range(DEPTH):
        m, j = i % N_MIXERS, i // N_MIXERS
        if i == 0:
            h, w = _ffn_first(x, front, (i, 0), ffn_g, w, (i, 1), ffn_f32)
        else:
            h, w = _ffn(h, (i, 0), ffn_g, w, (i, 1), ffn_f32)
        if m == 0:
            h = _mixa(h, i, j, *mixa_params)
        elif m == 1:
            h = _mixb(h, mix_norm[i], b_w_in[j].astype(BF16), b_conv[j], b_conv_bias[j],
                      b_ln_g[j], b_ln_b[j], b_w_out[j].astype(BF16))
        else:
            assert i < DEPTH - 1
            o, wo = _fox(h, mix_norm[i], c_w_in[j], c_b_f[j], c_q_norm[j], c_k_norm[j], c_w_out[j])
            h, w = _ffn_after_attn(h, o, wo, (i, 1), ffn_g, w, (i + 1, 0), ffn_f32)
            continue
        if i == DEPTH - 1:
            h = _ffn_last(h, (i, 1), ffn_g, w)
        else:
            h, w = _ffn(h, (i, 1), ffn_g, w, (i + 1, 0), ffn_f32)
    return h
```
